```python
import math
import jax
import jax.numpy as jnp
from jax import lax
import numpy as np

D_MODEL = 1024
BATCH = 2
SEQ = 8192
DEPTH = 4

HEAD_DIM = 64
N_GROUPS = 4
GROUP_WIDTH = D_MODEL // N_GROUPS
H_FOX = GROUP_WIDTH // HEAD_DIM
H_DIFF = GROUP_WIDTH // HEAD_DIM
DIFF_QK_DIM = HEAD_DIM // 2
H_DSA = GROUP_WIDTH // HEAD_DIM
H_IDX = 4
D_IDX = 64
K_SEL_MAX = 256
CONV_CH = GROUP_WIDTH
CONV_WIDTH = 31
Q_BLOCK = 128
NUM_BUCKETS = 32
MAX_DISTANCE = 128
N_BIAS_HEADS = H_DIFF + H_DSA
N_EXPERTS = 32
TOP_K = 4
D_FF = D_MODEL
SWIGLU_LIMIT = 7.0
SWIGLU_ALPHA = 1.702
FORGET_BIAS_INIT = 4.0
DEEPNORM_ALPHA = (2 * DEPTH) ** 0.25
DEEPNORM_BETA = (8 * DEPTH) ** -0.25
LN_EPS = 1e-5

IN_WIDTHS = (GROUP_WIDTH, GROUP_WIDTH, GROUP_WIDTH, H_FOX,
             GROUP_WIDTH, GROUP_WIDTH, GROUP_WIDTH,
             GROUP_WIDTH, GROUP_WIDTH, GROUP_WIDTH, H_IDX * D_IDX, D_IDX, H_IDX,
             2 * CONV_CH)
N_IN = sum(IN_WIDTHS)
SPLIT_POINTS = tuple(sum(IN_WIDTHS[:i + 1]) for i in range(len(IN_WIDTHS) - 1))
V_SLOTS = (2, 6, 9)
F_SLOT = 3

kernel_name = "hybrid_fox_diff_dsa_conformer_moe_deepnorm"


def _to_blocks(a):
    b, l = a.shape[0], a.shape[1]
    a = a.reshape((b, l // Q_BLOCK, Q_BLOCK) + a.shape[2:])
    return jnp.swapaxes(a, 0, 1)


def _from_blocks(a):
    a = jnp.swapaxes(a, 0, 1)
    return a.reshape((a.shape[0], a.shape[1] * a.shape[2]) + a.shape[3:])


def _layernorm(x, g, b):
    xf = x.astype(jnp.float32)
    mu = jnp.mean(xf, axis=-1, keepdims=True)
    var = jnp.mean(jnp.square(xf - mu), axis=-1, keepdims=True)
    y = (xf - mu) * lax.rsqrt(var + LN_EPS)
    return (y * g.astype(jnp.float32) + b.astype(jnp.float32)).astype(x.dtype)


def _rel_bucket(dist):
    n = jnp.maximum(dist, 0)
    max_exact = NUM_BUCKETS // 2
    nf = jnp.maximum(n, 1).astype(jnp.float32)
    large = max_exact + (jnp.log(nf / max_exact) / math.log(MAX_DISTANCE / max_exact)
                         * (NUM_BUCKETS - max_exact)).astype(jnp.int32)
    large = jnp.minimum(large, NUM_BUCKETS - 1)
    return jnp.where(n < max_exact, n, large)


def forgetting_attention(q, k, v, log_f):
    L = q.shape[1]
    c = jnp.cumsum(log_f.astype(jnp.float32), axis=1)
    c_k = jnp.swapaxes(c, 1, 2)[:, :, None, :]
    k_pos = jnp.arange(L, dtype=jnp.int32)
    scale = HEAD_DIM ** -0.5

    def block(args):
        qb, cb, i = args
        q_pos = i * Q_BLOCK + jnp.arange(Q_BLOCK, dtype=jnp.int32)
        s = jnp.einsum("bqhd,bkhd->bhqk", qb, k).astype(jnp.float32) * scale
        s = s + jnp.swapaxes(cb, 1, 2)[..., None] - c_k
        s = jnp.where(k_pos[None, :] <= q_pos[:, None], s, -jnp.inf)
        p = jax.nn.softmax(s, axis=-1).astype(v.dtype)
        return jnp.einsum("bhqk,bkhd->bqhd", p, v)

    nb = L // Q_BLOCK
    out = lax.map(block, (_to_blocks(q), _to_blocks(c), jnp.arange(nb, dtype=jnp.int32)))
    return _from_blocks(out)


def differential_attention(q, k, v, lam, bias_table):
    L = q.shape[1]
    k_pos = jnp.arange(L, dtype=jnp.int32)
    scale = DIFF_QK_DIM ** -0.5

    def block(args):
        qb, i = args
        q_pos = i * Q_BLOCK + jnp.arange(Q_BLOCK, dtype=jnp.int32)
        s = jnp.einsum("bqhmd,bkhmd->bhmqk", qb, k).astype(jnp.float32) * scale
        bias = bias_table[_rel_bucket(q_pos[:, None] - k_pos[None, :])]
        s = s + jnp.transpose(bias, (2, 0, 1))[None, :, None].astype(jnp.float32)
        s = jnp.where(k_pos[None, :] <= q_pos[:, None], s, -jnp.inf)
        p = jax.nn.softmax(s, axis=-1)
        p = p[:, :, 0] - lam * p[:, :, 1]
        return jnp.einsum("bhqk,bkhd->bqhd", p.astype(v.dtype), v)

    nb = L // Q_BLOCK
    out = lax.map(block, (_to_blocks(q), jnp.arange(nb, dtype=jnp.int32)))
    return _from_blocks(out)


def indexed_sparse_attention(q, k, v, q_idx, k_idx, w_idx, bias_table, top_k):
    L = q.shape[1]
    k_pos = jnp.arange(L, dtype=jnp.int32)
    scale = HEAD_DIM ** -0.5

    def block(args):
        qb, qib, wb, i = args
        q_pos = i * Q_BLOCK + jnp.arange(Q_BLOCK, dtype=jnp.int32)
        sc = jnp.einsum("bqhd,bkd->bqhk", qib, k_idx).astype(jnp.float32) * (D_IDX ** -0.5)
        score = jnp.einsum("bqh,bqhk->bqk", wb.astype(jnp.float32), jax.nn.relu(sc)) * (H_IDX ** -0.5)
        score = jnp.where(k_pos[None, None, :] <= q_pos[None, :, None], score, -jnp.inf)
        _, idx = lax.top_k(score, top_k)
        valid = idx <= q_pos[None, :, None]
        kg = jax.vmap(lambda a, j: a[j])(k, idx)
        vg = jax.vmap(lambda a, j: a[j])(v, idx)
        s = jnp.einsum("bqhd,bqkhd->bhqk", qb, kg).astype(jnp.float32) * scale
        bias = bias_table[_rel_bucket(q_pos[None, :, None] - idx)]
        s = s + jnp.transpose(bias, (0, 3, 1, 2)).astype(jnp.float32)
        s = jnp.where(valid[:, None], s, -jnp.inf)
        p = jax.nn.softmax(s, axis=-1).astype(v.dtype)
        return jnp.einsum("bhqk,bqkhd->bqhd", p, vg)

    nb = L // Q_BLOCK
    out = lax.map(block, (_to_blocks(q), _to_blocks(q_idx), _to_blocks(w_idx),
                          jnp.arange(nb, dtype=jnp.int32)))
    return _from_blocks(out)


def conformer_conv(u, conv_w, conv_b, ln_g, ln_b):
    a, g = jnp.split(u, 2, axis=-1)
    h = a * jax.nn.sigmoid(g)
    h = lax.conv_general_dilated(h, conv_w[:, None, :], window_strides=(1,),
                                 padding=((CONV_WIDTH - 1, 0),),
                                 dimension_numbers=("NWC", "WIO", "NWC"),
                                 feature_group_count=CONV_CH) + conv_b
    h = _layernorm(h, ln_g, ln_b)
    return h * jax.nn.sigmoid(h)


def token_mixers(x, rel_bias, w_in, forget_b, diff_lambda, diff_norm_g, conv_w, conv_b,
                 conv_ln_g, conv_ln_b, w_out, lambda_init):
    B, L, _ = x.shape
    proj = x @ w_in
    (fq, fk, fv, ff, dq, dk, dv, sq, sk, sv, iq, ik, iw, cu) = jnp.split(proj, SPLIT_POINTS, axis=-1)

    log_f = jax.nn.log_sigmoid((ff + forget_b).astype(jnp.float32))
    y_fox = forgetting_attention(fq.reshape(B, L, H_FOX, HEAD_DIM), fk.reshape(B, L, H_FOX, HEAD_DIM),
                                 fv.reshape(B, L, H_FOX, HEAD_DIM), log_f)

    lp = diff_lambda.astype(jnp.float32)
    lam = jnp.exp(jnp.sum(lp[0] * lp[1])) - jnp.exp(jnp.sum(lp[2] * lp[3])) + lambda_init
    y_diff = differential_attention(dq.reshape(B, L, H_DIFF, 2, DIFF_QK_DIM),
                                    dk.reshape(B, L, H_DIFF, 2, DIFF_QK_DIM),
                                    dv.reshape(B, L, H_DIFF, HEAD_DIM), lam, rel_bias[:, :H_DIFF])
    yf = y_diff.astype(jnp.float32)
    yf = yf * lax.rsqrt(jnp.mean(jnp.square(yf), axis=-1, keepdims=True) + LN_EPS)
    y_diff = (yf * diff_norm_g.astype(jnp.float32) * (1.0 - lambda_init)).astype(x.dtype)

    top_k = min(K_SEL_MAX, L // 4)
    y_dsa = indexed_sparse_attention(sq.reshape(B, L, H_DSA, HEAD_DIM), sk.reshape(B, L, H_DSA, HEAD_DIM),
                                     sv.reshape(B, L, H_DSA, HEAD_DIM), iq.reshape(B, L, H_IDX, D_IDX),
                                     ik, iw, rel_bias[:, H_DIFF:], top_k)

    y_conv = conformer_conv(cu, conv_w, conv_b, conv_ln_g, conv_ln_b)

    y = jnp.concatenate([y_fox.reshape(B, L, GROUP_WIDTH), y_diff.reshape(B, L, GROUP_WIDTH),
                         y_dsa.reshape(B, L, GROUP_WIDTH), y_conv], axis=-1)
    return y @ w_out


def moe_ffn(x, router_w, router_b, w_gu, b_gu, w_down, b_down):
    B, L, D = x.shape
    t = x.reshape(B * L, D)
    logits = (t @ router_w + router_b).astype(jnp.float32)
    top_val, top_idx = lax.top_k(logits, TOP_K)
    wts = jax.nn.softmax(top_val, axis=-1)
    gates = jnp.sum(jax.nn.one_hot(top_idx, N_EXPERTS, dtype=jnp.float32) * wts[..., None], axis=1)
    out = jnp.zeros((B * L, D), jnp.float32)
    for e in range(N_EXPERTS):
        gu = t @ w_gu[e] + b_gu[e]
        gate = jnp.minimum(gu[:, ::2], SWIGLU_LIMIT)
        up = jnp.clip(gu[:, 1::2], -SWIGLU_LIMIT, SWIGLU_LIMIT)
        h = (up + 1.0) * gate * jax.nn.sigmoid(gate * SWIGLU_ALPHA)
        out = out + gates[:, e:e + 1] * (h @ w_down[e] + b_down[e])
    return out.astype(x.dtype).reshape(B, L, D)


def setup_inputs(seed: int = 0) -> dict:
    key = jax.random.key(seed)
    ks = jax.random.split(key, 22)

    def nrm(k, shape, scale):
        return jax.random.normal(k, shape, jnp.float32) * scale

    slot_scale = [1.0] * len(IN_WIDTHS)
    for s in V_SLOTS:
        slot_scale[s] = DEEPNORM_BETA
    slot_scale[F_SLOT] = 0.1
    col_scale = jnp.asarray(np.concatenate(
        [np.full((w,), sc, np.float32) for w, sc in zip(IN_WIDTHS, slot_scale)]))

    return {
        "x": nrm(ks[0], (BATCH, SEQ, D_MODEL), 1.0),
        "rel_bias": nrm(ks[1], (NUM_BUCKETS, N_BIAS_HEADS), 0.2),
        "w_in": nrm(ks[2], (DEPTH, D_MODEL, N_IN), D_MODEL ** -0.5) * col_scale,
        "forget_b": FORGET_BIAS_INIT + nrm(ks[3], (DEPTH, H_FOX), 0.1),
        "diff_lambda": nrm(ks[4], (DEPTH, 4, DIFF_QK_DIM), 0.1),
        "diff_norm_g": 1.0 + nrm(ks[5], (DEPTH, HEAD_DIM), 0.01),
        "conv_w": nrm(ks[6], (DEPTH, CONV_WIDTH, CONV_CH), CONV_WIDTH ** -0.5),
        "conv_b": nrm(ks[7], (DEPTH, CONV_CH), 0.01),
        "conv_ln_g": 1.0 + nrm(ks[8], (DEPTH, CONV_CH), 0.01),
        "conv_ln_b": nrm(ks[9], (DEPTH, CONV_CH), 0.01),
        "w_out": nrm(ks[10], (DEPTH, D_MODEL, D_MODEL), D_MODEL ** -0.5 * DEEPNORM_BETA),
        "ln1_g": 1.0 + nrm(ks[11], (DEPTH, D_MODEL), 0.01),
        "ln1_b": nrm(ks[12], (DEPTH, D_MODEL), 0.01),
        "router_w": nrm(ks[13], (DEPTH, D_MODEL, N_EXPERTS), D_MODEL ** -0.5),
        "router_b": nrm(ks[14], (DEPTH, N_EXPERTS), 0.01),
        "w_gu": nrm(ks[15], (DEPTH, N_EXPERTS, D_MODEL, 2 * D_FF), D_MODEL ** -0.5 * DEEPNORM_BETA),
        "b_gu": nrm(ks[16], (DEPTH, N_EXPERTS, 2 * D_FF), 0.01),
        "w_down": nrm(ks[17], (DEPTH, N_EXPERTS, D_FF, D_MODEL), D_FF ** -0.5 * DEEPNORM_BETA),
        "b_down": nrm(ks[18], (DEPTH, N_EXPERTS, D_MODEL), 0.01),
        "ln2_g": 1.0 + nrm(ks[19], (DEPTH, D_MODEL), 0.01),
        "ln2_b": nrm(ks[20], (DEPTH, D_MODEL), 0.01),
    }


def reference(x, rel_bias, w_in, forget_b, diff_lambda, diff_norm_g, conv_w, conv_b, conv_ln_g,
              conv_ln_b, w_out, ln1_g, ln1_b, router_w, router_b, w_gu, b_gu, w_down, b_down,
              ln2_g, ln2_b):
    for l in range(DEPTH):
        lambda_init = 0.8 - 0.6 * math.exp(-0.3 * l)
        mix = token_mixers(x, rel_bias, w_in[l], forget_b[l], diff_lambda[l], diff_norm_g[l],
                           conv_w[l], conv_b[l], conv_ln_g[l], conv_ln_b[l], w_out[l], lambda_init)
        x = _layernorm(DEEPNORM_ALPHA * x + mix, ln1_g[l], ln1_b[l])
        ffn = moe_ffn(x, router_w[l], router_b[l], w_gu[l], b_gu[l], w_down[l], b_down[l])
        x = _layernorm(DEEPNORM_ALPHA * x + ffn, ln2_g[l], ln2_b[l])
    return x
```

```python
import functools
import math

import numpy as np
import jax
import jax.numpy as jnp
from jax import lax
from jax.experimental import pallas as pl
from jax.experimental.pallas import tpu as pltpu

F32 = jnp.float32
BF16 = jnp.bfloat16
I32 = jnp.int32

D_MODEL = 1024
HEAD_DIM = 64
N_HEADS = 4
GROUP = N_HEADS * HEAD_DIM
DIFF_QK = HEAD_DIM // 2
D_IDX = 64
K_SEL_MAX = 256
CONV_W = 31
CONV_HALO = 32
NUM_BUCKETS = 32
MAX_DISTANCE = 128
N_EXPERTS = 32
TOP_K = 4
SWIGLU_LIMIT = 7.0
SWIGLU_ALPHA = 1.702
LN_EPS = 1e-5
NEG = -1e30
INT_MIN = -(2 ** 31)
INT_MAX = 2 ** 31 - 1
LANE = 128
VMEM_LIMIT = 56 * 1024 * 1024

_IN_WIDTHS = (GROUP, GROUP, GROUP, N_HEADS, GROUP, GROUP, GROUP, GROUP, GROUP, GROUP,
              N_HEADS * D_IDX, D_IDX, N_HEADS, 2 * GROUP)
_IN_OFF = tuple(int(v) for v in np.cumsum((0,) + _IN_WIDTHS))
(_FQ, _FK, _FV, _FF, _DQ, _DK, _DV, _SQ, _SK, _SV, _IQ, _IK, _IW, _CU) = range(14)

NA = 10 * GROUP + LANE
NB = 2 * GROUP + LANE
GATE_COL = 2 * GROUP
A_COL = {"fq": 0, "fk": 1, "fv": 2, "dq": 3, "dk": 4, "dv": 5, "sq": 6, "sk": 7, "sv": 8, "iq": 9}
IK_BLOCK = 10 * GROUP // LANE


def _proj_perm():
    idx, scale = [], []

    def add(slot, s=1.0):
        idx.extend(range(_IN_OFF[slot], _IN_OFF[slot + 1]))
        scale.extend([s] * _IN_WIDTHS[slot])

    def pad(n):
        idx.extend([0] * n)
        scale.extend([0.0] * n)

    qs = HEAD_DIM ** -0.5
    for slot, s in ((_FQ, qs), (_FK, 1.0), (_FV, 1.0), (_DQ, 1.0), (_DK, 1.0), (_DV, 1.0),
                    (_SQ, qs), (_SK, 1.0), (_SV, 1.0), (_IQ, D_IDX ** -0.5), (_IK, 1.0)):
        add(slot, s)
    pad(LANE - D_IDX)
    add(_CU)
    add(_FF)
    add(_IW, N_HEADS ** -0.5)
    pad(LANE - 2 * N_HEADS)
    return np.asarray(idx, np.int32), np.asarray(scale, np.float32)


def _cparams(sem):
    return pltpu.CompilerParams(dimension_semantics=sem, vmem_limit_bytes=VMEM_LIMIT)


def _dot(a, b):
    return jnp.dot(a, b, preferred_element_type=F32)


def _dot_nt(a, b):
    return lax.dot_general(a, b, (((1,), (1,)), ((), ())), preferred_element_type=F32)


def _split3(a):
    a1 = a.astype(BF16)
    r1 = a - a1.astype(F32)
    a2 = r1.astype(BF16)
    a3 = (r1 - a2.astype(F32)).astype(BF16)
    return a1, a2, a3


def _layernorm(z, g, b):
    mu = jnp.mean(z, axis=-1, keepdims=True)
    zc = z - mu
    var = jnp.mean(zc * zc, axis=-1, keepdims=True)
    return zc * lax.rsqrt(var + LN_EPS) * g + b


def _proj_kernel(x_ref, w_ref, pa_ref, pb_ref):
    xb = x_ref[...].astype(BF16)
    pa_ref[...] = _dot(xb, w_ref[:, :NA]).astype(pa_ref.dtype)
    pb_ref[...] = _dot(xb, w_ref[:, NA:])


def _proj(x2d, w_r, tm):
    T = x2d.shape[0]
    return pl.pallas_call(
        _proj_kernel,
        grid=(T // tm,),
        in_specs=[pl.BlockSpec((tm, D_MODEL), lambda i: (i, 0)),
                  pl.BlockSpec((D_MODEL, NA + NB), lambda i: (0, 0))],
        out_specs=[pl.BlockSpec((tm, NA), lambda i: (i, 0)),
                   pl.BlockSpec((tm, NB), lambda i: (i, 0))],
        out_shape=[jax.ShapeDtypeStruct((T, NA), BF16), jax.ShapeDtypeStruct((T, NB), F32)],
        compiler_params=_cparams(("parallel",)),
        name="proj",
    )(x2d, w_r)


def _foxgate_kernel(g_ref, fb_ref, c_ref, *, chunk):
    L = g_ref.shape[0]
    row = lax.broadcasted_iota(I32, (chunk, chunk), 0)
    col = lax.broadcasted_iota(I32, (chunk, chunk), 1)
    tri = jnp.where(col <= row, 1.0, 0.0).astype(BF16)

    def body(i, carry):
        r0 = pl.multiple_of(i * chunk, chunk)
        z = g_ref[pl.ds(r0, chunk), :] + fb_ref[...]
        lf = jnp.minimum(z, 0.0) - jnp.log1p(jnp.exp(-jnp.abs(z)))
        p1, p2, p3 = _split3(lf)
        cs = _dot(tri, p1) + _dot(tri, p2) + _dot(tri, p3) + carry
        c_ref[pl.ds(r0, chunk), :] = cs
        return cs[chunk - 1:chunk, :]

    lax.fori_loop(0, L // chunk, body, jnp.zeros((1, LANE), F32))


def _foxgate(pb, fb_row, B, L):
    return pl.pallas_call(
        functools.partial(_foxgate_kernel, chunk=256),
        grid=(B,),
        in_specs=[pl.BlockSpec((L, LANE), lambda b: (b, GATE_COL // LANE)),
                  pl.BlockSpec((1, LANE), lambda b: (0, 0))],
        out_specs=pl.BlockSpec((L, LANE), lambda b: (b, 0)),
        out_shape=jax.ShapeDtypeStruct((B * L, LANE), F32),
        compiler_params=_cparams(("parallel",)),
        name="foxgate",
    )(pb, fb_row)


def _online_update(carry, s, vh):
    m, l, acc = carry
    m_new = jnp.maximum(m, jnp.max(s, axis=1, keepdims=True))
    alpha = jnp.exp(m - m_new)
    p = jnp.exp(s - m_new)
    l = alpha * l + jnp.sum(p, axis=1, keepdims=True)
    acc = alpha * acc + _dot(p.astype(BF16), vh)
    return m_new, l, acc


def _softmax_init(tq):
    return (jnp.full((tq, 1), NEG, F32), jnp.zeros((tq, 1), F32), jnp.zeros((tq, HEAD_DIM), F32))


def _causal(tq, tk):
    return lax.broadcasted_iota(I32, (tq, tk), 1) <= lax.broadcasted_iota(I32, (tq, tk), 0)


def _attn_specs(L, tq, names):
    nq = L // tq
    specs = [pl.BlockSpec((tq, GROUP), functools.partial(lambda c, b, i: (b * nq + i, c), A_COL[names[0]]))]
    for n in names[1:]:
        specs.append(pl.BlockSpec((L, GROUP), functools.partial(lambda c, b, i: (b, c), A_COL[n])))
    return specs


def _fox_kernel(q_ref, k_ref, v_ref, ck_ref, o_ref, *, t):
    i = pl.program_id(1)
    causal = _causal(t, t)
    for h in range(N_HEADS):
        sl = slice(h * HEAD_DIM, (h + 1) * HEAD_DIM)
        qh = q_ref[:, sl]

        def step(j, carry, masked, qh=qh, sl=sl, h=h):
            k0 = pl.multiple_of(j * t, t)
            s = _dot_nt(qh, k_ref[pl.ds(k0, t), sl]) - ck_ref[0, j][h:h + 1, :]
            if masked:
                s = jnp.where(causal, s, NEG)
            return _online_update(carry, s, v_ref[pl.ds(k0, t), sl])

        carry = lax.fori_loop(0, i, functools.partial(step, masked=False), _softmax_init(t))
        _, l, acc = step(i, carry, True)
        o_ref[:, sl] = (acc / l).astype(o_ref.dtype)


def _fox(pa, ck, B, L, t):
    nq = L // t
    return pl.pallas_call(
        functools.partial(_fox_kernel, t=t),
        grid=(B, nq),
        in_specs=_attn_specs(L, t, ("fq", "fk", "fv"))
        + [pl.BlockSpec((1, nq, 8, t), lambda b, i: (b, 0, 0, 0))],
        out_specs=pl.BlockSpec((t, GROUP), lambda b, i: (b * nq + i, 0)),
        out_shape=jax.ShapeDtypeStruct((B * L, GROUP), BF16),
        compiler_params=_cparams(("parallel", "arbitrary")),
        name="fox",
    )(pa, pa, pa, ck)


def _diff_kernel(lam_ref, q_ref, k_ref, v_ref, dtab_ref, g_ref, o_ref, *, t, out_scale):
    i = pl.program_id(1)
    lam = lam_ref[0]
    scale = DIFF_QK ** -0.5
    for h in range(N_HEADS):
        c0 = h * HEAD_DIM
        q1 = q_ref[:, c0:c0 + DIFF_QK]
        q2 = q_ref[:, c0 + DIFF_QK:c0 + HEAD_DIM]

        def step(j, carry, near, q1=q1, q2=q2, c0=c0, h=h):
            k0 = pl.multiple_of(j * t, t)
            kk = k_ref[pl.ds(k0, t), c0:c0 + HEAD_DIM]
            vh = v_ref[pl.ds(k0, t), c0:c0 + HEAD_DIM]
            s1 = _dot_nt(q1, kk[:, :DIFF_QK]) * scale
            s2 = _dot_nt(q2, kk[:, DIFF_QK:]) * scale
            if near:
                bias = dtab_ref[h, i - j]
                s1 = s1 + bias
                s2 = s2 + bias
            return _online_update(carry[0], s1, vh), _online_update(carry[1], s2, vh)

        n_far = jnp.maximum(i - 1, 0)
        carry = lax.fori_loop(0, n_far, functools.partial(step, near=False),
                              (_softmax_init(t), _softmax_init(t)))
        (_, l1, a1), (_, l2, a2) = lax.fori_loop(n_far, i + 1, functools.partial(step, near=True), carry)
        o = a1 / l1 - lam * (a2 / l2)
        o = o * lax.rsqrt(jnp.mean(o * o, axis=-1, keepdims=True) + LN_EPS)
        o_ref[:, c0:c0 + HEAD_DIM] = (o * g_ref[...] * out_scale).astype(o_ref.dtype)


def _diff(lam, pa, dtab, g_row, B, L, t, out_scale):
    nq = L // t
    return pl.pallas_call(
        functools.partial(_diff_kernel, t=t, out_scale=out_scale),
        grid=(B, nq),
        in_specs=[pl.BlockSpec(memory_space=pltpu.SMEM)]
        + _attn_specs(L, t, ("dq", "dk", "dv"))
        + [pl.BlockSpec((N_HEADS, 2, t, t), lambda b, i: (0, 0, 0, 0)),
           pl.BlockSpec((1, HEAD_DIM), lambda b, i: (0, 0))],
        out_specs=pl.BlockSpec((t, GROUP), lambda b, i: (b * nq + i, 0)),
        out_shape=jax.ShapeDtypeStruct((B * L, GROUP), BF16),
        compiler_params=_cparams(("parallel", "arbitrary")),
        name="diff",
    )(lam, pa, pa, pa, dtab, g_row)


def _floor_avg(a, b):
    return (a >> 1) + (b >> 1) + (a & b & 1)


def _dsa_kernel(q_ref, k_ref, v_ref, iq_ref, ik_ref, gate_ref, dtab_ref, o_ref, keys_ref, *, t, top_k, L):
    i = pl.program_id(1)
    nblk = i + 1
    causal = _causal(t, t)
    lane_pos = lax.broadcasted_iota(I32, (t, t), 1)
    row_pos = i * t + lax.broadcasted_iota(I32, (t, 1), 0)

    iw = gate_ref[...]

    def score_block(c, _):
        k0 = pl.multiple_of(c * t, t)
        ikc = ik_ref[pl.ds(k0, t), :][:, :D_IDX]
        acc = None
        for h in range(N_HEADS):
            a = _dot_nt(iq_ref[:, h * D_IDX:(h + 1) * D_IDX], ikc)
            term = jnp.maximum(a, 0.0) * iw[:, N_HEADS + h:N_HEADS + h + 1]
            acc = term if acc is None else acc + term
        bits = lax.bitcast_convert_type(acc, I32)
        key = jnp.where(bits < 0, bits ^ INT_MAX, bits)
        keys_ref[c] = key - (key >> 31)
        return 0

    lax.fori_loop(0, nblk, score_block, 0)
    keys_ref[i] = jnp.where(causal, keys_ref[i], INT_MIN)

    def count(pred):
        def body(c, acc):
            ind = pred(keys_ref[c], c)
            part = ind[:, :LANE]
            for u in range(1, t // LANE):
                part = part + ind[:, u * LANE:(u + 1) * LANE]
            return acc + part
        acc = lax.fori_loop(0, nblk, body, jnp.zeros((t, LANE), F32))
        return jnp.sum(acc, axis=1, keepdims=True).astype(I32)

    n_valid = row_pos + 1
    small = n_valid <= top_k
    lo0 = jnp.full((t, 1), INT_MIN + 1, I32)
    hi0 = jnp.full((t, 1), INT_MAX, I32)
    state0 = dict(lo=lo0, hi=hi0, clo=n_valid, chi=jnp.zeros((t, 1), I32),
                  thr=lo0, done=small.astype(I32))

    def cond(c):
        return c[0] > 0

    def body(c):
        _, st = c
        lo, hi, clo, chi, thr, done = st["lo"], st["hi"], st["clo"], st["chi"], st["thr"], st["done"]
        mid = _floor_avg(lo, hi)
        mid = jnp.where((lo == 0) & (hi == INT_MAX), 1, mid)
        cnt = count(lambda kc, _c: jnp.where(kc >= mid, 1.0, 0.0))
        act = done == 0
        ge = cnt >= top_k
        lo_n = jnp.where(ge, mid, lo)
        clo_n = jnp.where(ge, cnt, clo)
        hi_n = jnp.where(ge, hi, mid)
        chi_n = jnp.where(ge, chi, cnt)
        exact = cnt == top_k
        collapsed = hi_n == lo_n + 1
        fin = exact | collapsed
        thr_n = jnp.where(exact, mid, lo_n)
        st_n = dict(lo=jnp.where(act, lo_n, lo), hi=jnp.where(act, hi_n, hi),
                    clo=jnp.where(act & ~exact, clo_n, jnp.where(act, top_k, clo)),
                    chi=jnp.where(act & ~exact, chi_n, jnp.where(act, 0, chi)),
                    thr=jnp.where(act & fin, thr_n, thr),
                    done=jnp.where(act & fin, 1, done))
        n_act = jnp.sum(1 - st_n["done"])
        return n_act, st_n

    n_act0 = jnp.sum(1 - state0["done"])
    _, st = lax.while_loop(cond, body, (n_act0, state0))
    thr = st["thr"]
    need = top_k - st["chi"]
    tied = (st["clo"] > top_k) & ~small

    def tie_cut(_):
        def pbody(_it, c):
            plo, phi = c
            pmid = (plo + phi) >> 1
            cnt = count(lambda kc, cc: jnp.where((kc == thr) & (lane_pos + cc * t <= pmid), 1.0, 0.0))
            ok = cnt >= need
            return jnp.where(ok, plo, pmid), jnp.where(ok, pmid, phi)
        n_it = int(math.ceil(math.log2(L))) + 1
        _, phi = lax.fori_loop(0, n_it, pbody, (jnp.full((t, 1), -1, I32), jnp.full((t, 1), L - 1, I32)))
        return jnp.where(tied, phi, L)

    jcut = lax.cond(jnp.sum(tied.astype(I32)) > 0, tie_cut, lambda _: jnp.full((t, 1), L, I32), 0)

    def attn_block(c, carry, near):
        k0 = pl.multiple_of(c * t, t)
        kc = keys_ref[c]
        sel = (kc + jnp.where(lane_pos + c * t <= jcut, 1, 0)) > thr
        out = []
        for h in range(N_HEADS):
            sl = slice(h * HEAD_DIM, (h + 1) * HEAD_DIM)
            s = _dot_nt(q_ref[:, sl], k_ref[pl.ds(k0, t), sl])
            if near:
                s = s + dtab_ref[h, i - c]
            s = jnp.where(sel, s, NEG)
            out.append(_online_update(carry[h], s, v_ref[pl.ds(k0, t), sl]))
        return tuple(out)

    n_far = jnp.maximum(i - 1, 0)
    carry = lax.fori_loop(0, n_far, functools.partial(attn_block, near=False),
                          tuple(_softmax_init(t) for _ in range(N_HEADS)))
    carry = lax.fori_loop(n_far, nblk, functools.partial(attn_block, near=True), carry)
    for h in range(N_HEADS):
        _, l, acc = carry[h]
        o_ref[:, h * HEAD_DIM:(h + 1) * HEAD_DIM] = (acc / l).astype(o_ref.dtype)


def _dsa(pa, pb, dtab, B, L, t, top_k):
    nq = L // t
    return pl.pallas_call(
        functools.partial(_dsa_kernel, t=t, top_k=top_k, L=L),
        grid=(B, nq),
        in_specs=_attn_specs(L, t, ("sq", "sk", "sv"))
        + [pl.BlockSpec((t, GROUP), lambda b, i: (b * nq + i, A_COL["iq"])),
           pl.BlockSpec((L, LANE), lambda b, i: (b, IK_BLOCK)),
           pl.BlockSpec((t, LANE), lambda b, i: (b * nq + i, GATE_COL // LANE)),
           pl.BlockSpec((N_HEADS, 2, t, t), lambda b, i: (0, 0, 0, 0))],
        out_specs=pl.BlockSpec((t, GROUP), lambda b, i: (b * nq + i, 0)),
        out_shape=jax.ShapeDtypeStruct((B * L, GROUP), BF16),
        scratch_shapes=[pltpu.VMEM((nq, t, t), I32)],
        compiler_params=_cparams(("parallel", "arbitrary")),
        name="dsa",
    )(pa, pa, pa, pa, pa, pb, dtab)


def _conv_kernel(u_ref, halo_ref, w_ref, cb_ref, g_ref, b_ref, o_ref, h_ref, *, tc):
    i = pl.program_id(1)

    def glu(u):
        return u[:, :GROUP] * jax.nn.sigmoid(u[:, GROUP:])

    h_ref[pl.ds(CONV_HALO, tc), :] = glu(u_ref[...])
    h_ref[pl.ds(0, CONV_HALO), :] = jnp.where(i > 0, glu(halo_ref[...]), 0.0)
    acc = jnp.zeros((tc, GROUP), F32)
    for j in range(CONV_W):
        acc = acc + h_ref[pl.ds(CONV_HALO - (CONV_W - 1) + j, tc), :] * w_ref[j:j + 1, :]
    y = _layernorm(acc + cb_ref[...], g_ref[...], b_ref[...])
    o_ref[...] = (y * jax.nn.sigmoid(y)).astype(o_ref.dtype)


def _conv(pb, conv_w, cb, g, b, B, L, tc):
    nc = L // tc
    r = tc // CONV_HALO
    row = lambda a: a.reshape(1, GROUP)
    return pl.pallas_call(
        functools.partial(_conv_kernel, tc=tc),
        grid=(B, nc),
        in_specs=[pl.BlockSpec((tc, 2 * GROUP), lambda bb, i: (bb * nc + i, 0)),
                  pl.BlockSpec((CONV_HALO, 2 * GROUP), lambda bb, i: (jnp.maximum((bb * nc + i) * r - 1, 0), 0)),
                  pl.BlockSpec((CONV_W, GROUP), lambda bb, i: (0, 0)),
                  pl.BlockSpec((1, GROUP), lambda bb, i: (0, 0)),
                  pl.BlockSpec((1, GROUP), lambda bb, i: (0, 0)),
                  pl.BlockSpec((1, GROUP), lambda bb, i: (0, 0))],
        out_specs=pl.BlockSpec((tc, GROUP), lambda bb, i: (bb * nc + i, 0)),
        out_shape=jax.ShapeDtypeStruct((B * L, GROUP), BF16),
        scratch_shapes=[pltpu.VMEM((tc + CONV_HALO, GROUP), F32)],
        compiler_params=_cparams(("parallel", "arbitrary")),
        name="conv",
    )(pb, pb, conv_w, row(cb), row(g), row(b))


def _mix_kernel(yf_ref, yd_ref, ys_ref, yc_ref, x_ref, wo_ref, g_ref, b_ref, rw_ref, rb_ref,
                x1_ref, idx_ref, wt_ref, *, alpha):
    mix = None
    for n, y_ref in enumerate((yf_ref, yd_ref, ys_ref, yc_ref)):
        part = _dot(y_ref[...], wo_ref[n * GROUP:(n + 1) * GROUP, :])
        mix = part if mix is None else mix + part
    x1 = _layernorm(alpha * x_ref[...] + mix, g_ref[...], b_ref[...])
    x1_ref[...] = x1

    a1, a2, a3 = _split3(x1)
    w1, w2, w3 = rw_ref[0], rw_ref[1], rw_ref[2]
    lg = (_dot(a1, w1) + (_dot(a1, w2) + _dot(a2, w1)) + (_dot(a1, w3) + _dot(a2, w2) + _dot(a3, w1))) + rb_ref[...]

    tm = lg.shape[0]
    lane = lax.broadcasted_iota(I32, (tm, LANE), 1)
    lane_f = lane.astype(F32)
    idx_out = jnp.zeros((tm, LANE), I32)
    vals = []
    for r in range(TOP_K):
        mx = jnp.max(lg, axis=1, keepdims=True)
        am = jnp.min(jnp.where(lg == mx, lane_f, float(LANE)), axis=1, keepdims=True).astype(I32)
        vals.append(mx)
        idx_out = jnp.where(lane == r, am, idx_out)
        lg = jnp.where(lane == am, 2 * NEG, lg)
    es = [jnp.exp(v - vals[0]) for v in vals]
    den = es[0] + es[1] + es[2] + es[3]
    wt = jnp.zeros((tm, LANE), F32)
    for r in range(TOP_K):
        wt = jnp.where(lane == r, es[r] / den, wt)
    idx_ref[...] = idx_out
    wt_ref[...] = wt


def _mix(ys, x2d, wo, g, b, rw3, rb, alpha, tm):
    T = x2d.shape[0]
    yspec = pl.BlockSpec((tm, GROUP), lambda i: (i, 0))
    full = lambda shp: pl.BlockSpec(shp, lambda i: (0,) * len(shp))
    return pl.pallas_call(
        functools.partial(_mix_kernel, alpha=alpha),
        grid=(T // tm,),
        in_specs=[yspec, yspec, yspec, yspec,
                  pl.BlockSpec((tm, D_MODEL), lambda i: (i, 0)),
                  full((D_MODEL, D_MODEL)), full((1, D_MODEL)), full((1, D_MODEL)),
                  full((3, D_MODEL, LANE)), full((1, LANE))],
        out_specs=[pl.BlockSpec((tm, D_MODEL), lambda i: (i, 0)),
                   pl.BlockSpec((tm, LANE), lambda i: (i, 0)),
                   pl.BlockSpec((tm, LANE), lambda i: (i, 0))],
        out_shape=[jax.ShapeDtypeStruct((T, D_MODEL), F32),
                   jax.ShapeDtypeStruct((T, LANE), I32),
                   jax.ShapeDtypeStruct((T, LANE), F32)],
        compiler_params=_cparams(("parallel",)),
        name="mix",
    )(*ys, x2d, wo, g.reshape(1, D_MODEL), b.reshape(1, D_MODEL), rw3, rb)


def _row_copy(src_hbm, dst_hbm, s, d, sem):
    return pltpu.make_async_copy(src_hbm.at[pl.ds(s, 1)], dst_hbm.at[pl.ds(d, 1)], sem)


def _gather_kernel(src_ref, x_hbm, o_hbm, sem, *, tg):
    base = pl.program_id(0) * tg

    def issue(r, _):
        _row_copy(x_hbm, o_hbm, src_ref[base + r], base + r, sem).start()
        return 0

    def drain(r, _):
        _row_copy(x_hbm, o_hbm, src_ref[base + r], base + r, sem).wait()
        return 0

    lax.fori_loop(0, tg, issue, 0)
    lax.fori_loop(0, tg, drain, 0)


def _gather_rows(src_tok, x2d, R, tg):
    return pl.pallas_call(
        functools.partial(_gather_kernel, tg=tg),
        grid_spec=pltpu.PrefetchScalarGridSpec(
            num_scalar_prefetch=1,
            grid=(R // tg,),
            in_specs=[pl.BlockSpec(memory_space=pl.ANY)],
            out_specs=pl.BlockSpec(memory_space=pl.ANY),
            scratch_shapes=[pltpu.SemaphoreType.DMA(())],
        ),
        out_shape=jax.ShapeDtypeStruct((R, D_MODEL), x2d.dtype),
        compiler_params=_cparams(("arbitrary",)),
        name="moe_gather",
    )(src_tok, x2d)


def _expert_kernel(te_ref, tv_ref, xs_ref, wg_ref, wu_ref, bg_ref, bu_ref, wd_ref, bd_ref, y_ref):
    i = pl.program_id(0)

    @pl.when(tv_ref[i] > 0)
    def _():
        xb = xs_ref[...].astype(BF16)
        gate = jnp.minimum(_dot(xb, wg_ref[0]) + bg_ref[0], SWIGLU_LIMIT)
        up = jnp.clip(_dot(xb, wu_ref[0]) + bu_ref[0], -SWIGLU_LIMIT, SWIGLU_LIMIT)
        h = (up + 1.0) * gate * jax.nn.sigmoid(gate * SWIGLU_ALPHA)
        y_ref[...] = _dot(h.astype(BF16), wd_ref[0]) + bd_ref[0]

    @pl.when(tv_ref[i] == 0)
    def _():
        y_ref[...] = jnp.zeros_like(y_ref)


def _experts(tile_e, tile_v, xs, wg, wu, bg, bu, wd, bd, tmm):
    R = xs.shape[0]
    wspec = pl.BlockSpec((1, D_MODEL, D_MODEL), lambda i, te, tv: (te[i], 0, 0))
    bspec = pl.BlockSpec((1, 1, D_MODEL), lambda i, te, tv: (te[i], 0, 0))
    return pl.pallas_call(
        _expert_kernel,
        grid_spec=pltpu.PrefetchScalarGridSpec(
            num_scalar_prefetch=2,
            grid=(R // tmm,),
            in_specs=[pl.BlockSpec((tmm, D_MODEL), lambda i, te, tv: (i, 0)),
                      wspec, wspec, bspec, bspec, wspec, bspec],
            out_specs=pl.BlockSpec((tmm, D_MODEL), lambda i, te, tv: (i, 0)),
        ),
        out_shape=jax.ShapeDtypeStruct((R, D_MODEL), F32),
        compiler_params=_cparams(("arbitrary",)),
        name="moe_experts",
    )(tile_e, tile_v, xs, wg, wu, bg, bu, wd, bd)


def _combine_kernel(pos_ref, ys_hbm, x1_ref, wt_ref, g_ref, b_ref, o_ref, buf, sem, *, tc, alpha):
    base = pl.program_id(0) * (tc * TOP_K)

    def copy(r):
        tok = r // TOP_K
        k = r - tok * TOP_K
        return pltpu.make_async_copy(ys_hbm.at[pl.ds(pos_ref[base + r], 1)], buf.at[k, pl.ds(tok, 1)], sem)

    def issue(r, _):
        copy(r).start()
        return 0

    def drain(r, _):
        copy(r).wait()
        return 0

    lax.fori_loop(0, tc * TOP_K, issue, 0)
    lax.fori_loop(0, tc * TOP_K, drain, 0)
    wt = wt_ref[...]
    ffn = buf[0] * wt[:, 0:1]
    for k in range(1, TOP_K):
        ffn = ffn + buf[k] * wt[:, k:k + 1]
    o_ref[...] = _layernorm(alpha * x1_ref[...] + ffn, g_ref[...], b_ref[...])


def _combine(pos, ys, x1, wt, g, b, alpha, tc):
    T = x1.shape[0]
    return pl.pallas_call(
        functools.partial(_combine_kernel, tc=tc, alpha=alpha),
        grid_spec=pltpu.PrefetchScalarGridSpec(
            num_scalar_prefetch=1,
            grid=(T // tc,),
            in_specs=[pl.BlockSpec(memory_space=pl.ANY),
                      pl.BlockSpec((tc, D_MODEL), lambda i, p: (i, 0)),
                      pl.BlockSpec((tc, LANE), lambda i, p: (i, 0)),
                      pl.BlockSpec((1, D_MODEL), lambda i, p: (0, 0)),
                      pl.BlockSpec((1, D_MODEL), lambda i, p: (0, 0))],
            out_specs=pl.BlockSpec((tc, D_MODEL), lambda i, p: (i, 0)),
            scratch_shapes=[pltpu.VMEM((TOP_K, tc, D_MODEL), F32), pltpu.SemaphoreType.DMA(())],
        ),
        out_shape=jax.ShapeDtypeStruct((T, D_MODEL), F32),
        compiler_params=_cparams(("arbitrary",)),
        name="moe_combine",
    )(pos, ys, x1, wt, g.reshape(1, D_MODEL), b.reshape(1, D_MODEL))


def _moe_plan(top_idx, tmm):
    T = top_idx.shape[0]
    F = T * TOP_K
    R = F + N_EXPERTS * tmm
    flat_e = top_idx.reshape(F)
    onehot = (flat_e[:, None] == jnp.arange(N_EXPERTS, dtype=I32)[None, :]).astype(I32)
    csum = jnp.cumsum(onehot, axis=0)
    counts = csum[-1]
    rank_in_e = jnp.take_along_axis(csum, flat_e[:, None], axis=1)[:, 0] - 1
    padded = ((counts + tmm - 1) // tmm) * tmm
    gend = jnp.cumsum(padded)
    gstart = gend - padded
    ustart = jnp.cumsum(counts) - counts
    pos = gstart[flat_e] + rank_in_e
    order = jnp.argsort(flat_e, stable=True).astype(I32)
    rows = jnp.arange(R, dtype=I32)
    e_row = jnp.minimum(jnp.searchsorted(gend, rows, side="right").astype(I32), N_EXPERTS - 1)
    off = rows - gstart[e_row]
    valid = off < counts[e_row]
    src = jnp.where(valid, order[jnp.clip(ustart[e_row] + off, 0, F - 1)] // TOP_K, 0)
    tile_rows = rows[::tmm]
    tile_v = (tile_rows < gend[-1]).astype(I32)
    tile_e = jnp.where(tile_v > 0, e_row[::tmm], e_row[jnp.maximum(gend[-1] - 1, 0)])
    return pos.astype(I32), src.astype(I32), tile_e.astype(I32), tile_v, R


def _rel_bucket(dist):
    n = jnp.maximum(dist, 0)
    max_exact = NUM_BUCKETS // 2
    nf = jnp.maximum(n, 1).astype(F32)
    large = max_exact + (jnp.log(nf / max_exact) / math.log(MAX_DISTANCE / max_exact)
                         * (NUM_BUCKETS - max_exact)).astype(I32)
    large = jnp.minimum(large, NUM_BUCKETS - 1)
    return jnp.where(n < max_exact, n, large)


def _bias_tables(table, t):
    assert t >= MAX_DISTANCE, "key blocks two or more away must lie entirely in the last bucket"
    d0 = jnp.arange(t, dtype=I32)[:, None] - jnp.arange(t, dtype=I32)[None, :]
    rel = table - table[NUM_BUCKETS - 1][None, :]
    out = []
    for d in (d0, d0 + t):
        b = jnp.transpose(rel[_rel_bucket(d)], (2, 0, 1))
        out.append(jnp.where((d >= 0)[None], b, NEG))
    return jnp.stack(out, axis=1).astype(F32)


def kernel(x, rel_bias, w_in, forget_b, diff_lambda, diff_norm_g, conv_w, conv_b, conv_ln_g, conv_ln_b,
           w_out, ln1_g, ln1_b, router_w, router_b, w_gu, b_gu, w_down, b_down, ln2_g, ln2_b):
    B, L, D = x.shape
    depth = w_in.shape[0]
    assert D == D_MODEL
    T = B * L
    t = 256
    tm = 256
    tmm = 256
    tcv = 512 if L % 512 == 0 else 256
    assert L % t == 0 and T % tm == 0
    top_k = min(K_SEL_MAX, L // 4)
    alpha = (2 * depth) ** 0.25
    nq = L // t

    perm, colscale = _proj_perm()
    dtab_diff = _bias_tables(rel_bias[:, :N_HEADS], t)
    dtab_dsa = _bias_tables(rel_bias[:, N_HEADS:], t)

    x2d = x.reshape(T, D)
    for l in range(depth):
        lambda_init = 0.8 - 0.6 * math.exp(-0.3 * l)
        w_r = (w_in[l][:, perm] * colscale[None, :]).astype(BF16)
        pa, pb = _proj(x2d, w_r, tm)

        fb_row = jnp.zeros((1, LANE), F32).at[0, :N_HEADS].set(forget_b[l])
        c = _foxgate(pb, fb_row, B, L)
        ck = jnp.transpose(c[:, :8].reshape(B, nq, t, 8), (0, 1, 3, 2))
        y_fox = _fox(pa, ck, B, L, t)

        lp = diff_lambda[l].astype(F32)
        lam = (jnp.exp(jnp.sum(lp[0] * lp[1])) - jnp.exp(jnp.sum(lp[2] * lp[3])) + lambda_init).reshape(1)
        y_diff = _diff(lam, pa, dtab_diff, diff_norm_g[l].reshape(1, HEAD_DIM), B, L, t, 1.0 - lambda_init)

        y_dsa = _dsa(pa, pb, dtab_dsa, B, L, t, top_k)
        y_conv = _conv(pb, conv_w[l], conv_b[l], conv_ln_g[l], conv_ln_b[l], B, L, tcv)

        rw = jnp.zeros((D, LANE), F32).at[:, :N_EXPERTS].set(router_w[l])
        r1 = rw.astype(BF16)
        r2 = (rw - r1.astype(F32)).astype(BF16)
        r3 = (rw - r1.astype(F32) - r2.astype(F32)).astype(BF16)
        rb = jnp.full((1, LANE), NEG, F32).at[0, :N_EXPERTS].set(router_b[l])
        x1, top_idx, top_w = _mix((y_fox, y_diff, y_dsa, y_conv), x2d, w_out[l].astype(BF16),
                                  ln1_g[l], ln1_b[l], jnp.stack([r1, r2, r3]), rb, alpha, tm)

        pos, src, tile_e, tile_v, R = _moe_plan(top_idx[:, :TOP_K], tmm)
        xs = _gather_rows(src, x1, R, 256)
        wg = w_gu[l][:, :, 0::2].astype(BF16)
        wu = w_gu[l][:, :, 1::2].astype(BF16)
        bg = b_gu[l][:, None, 0::2]
        bu = b_gu[l][:, None, 1::2]
        ys = _experts(tile_e, tile_v, xs, wg, wu, bg, bu, w_down[l].astype(BF16), b_down[l][:, None, :], tmm)
        x2d = _combine(pos, ys, x1, top_w, ln2_g[l], ln2_b[l], alpha, 128)
    return x2d.reshape(B, L, D)
```

```python
import functools
import math

import numpy as np
import jax
import jax.numpy as jnp
from jax import lax
from jax.experimental import pallas as pl
from jax.experimental.pallas import tpu as pltpu

F32 = jnp.float32
BF16 = jnp.bfloat16
I32 = jnp.int32

D_MODEL = 1024
HEAD_DIM = 64
N_HEADS = 4
GROUP = N_HEADS * HEAD_DIM
DIFF_QK = HEAD_DIM // 2
D_IDX = 64
K_SEL_MAX = 256
CONV_W = 31
CONV_HALO = 32
NUM_BUCKETS = 32
MAX_DISTANCE = 128
N_EXPERTS = 32
TOP_K = 4
SWIGLU_LIMIT = 7.0
SWIGLU_ALPHA = 1.702
LN_EPS = 1e-5
NEG = -1e30
INT_MIN = -(2 ** 31)
INT_MAX = 2 ** 31 - 1
LANE = 128
VMEM_LIMIT = 56 * 1024 * 1024

_IN_WIDTHS = (GROUP, GROUP, GROUP, N_HEADS, GROUP, GROUP, GROUP, GROUP, GROUP, GROUP,
              N_HEADS * D_IDX, D_IDX, N_HEADS, 2 * GROUP)
_IN_OFF = tuple(int(v) for v in np.cumsum((0,) + _IN_WIDTHS))
(_FQ, _FK, _FV, _FF, _DQ, _DK, _DV, _SQ, _SK, _SV, _IQ, _IK, _IW, _CU) = range(14)

QT_GROUP = {"f": 0, "d": 1, "s": 2, "i": 3}
KA_BLOCK = {"f": 0, "d": 1, "s": 2}
NK = 3 * GROUP + LANE
IK_BLOCK = 3 * GROUP // LANE
NB = 2 * GROUP + LANE
GATE_BLOCK = 2 * GROUP // LANE
GT_ROWS = 16


def _slot_cols(slot):
    return list(range(_IN_OFF[slot], _IN_OFF[slot + 1]))


def _proj_weights(w):
    qs = HEAD_DIM ** -0.5
    q_idx = _slot_cols(_FQ) + _slot_cols(_DQ) + _slot_cols(_SQ) + _slot_cols(_IQ)
    q_scale = np.repeat(np.asarray([qs, 1.0, qs, D_IDX ** -0.5], np.float32), GROUP)
    wq_t = (w[:, np.asarray(q_idx)] * q_scale[None, :]).T.astype(BF16)
    k_idx = _slot_cols(_FK) + _slot_cols(_DK) + _slot_cols(_SK) + _slot_cols(_IK)
    wk = jnp.pad(w[:, np.asarray(k_idx)], ((0, 0), (0, LANE - D_IDX))).astype(BF16)
    v_idx = _slot_cols(_FV) + _slot_cols(_DV) + _slot_cols(_SV)
    wv_t = w[:, np.asarray(v_idx)].T.astype(BF16)
    iw_scale = N_HEADS ** -0.5
    b_idx = _slot_cols(_CU) + _slot_cols(_FF) + _slot_cols(_IW)
    b_scale = np.concatenate([np.ones(2 * GROUP + N_HEADS, np.float32), np.full(N_HEADS, iw_scale, np.float32)])
    wb = jnp.pad(w[:, np.asarray(b_idx)] * b_scale[None, :], ((0, 0), (0, LANE - 2 * N_HEADS))).astype(BF16)
    wg_t = jnp.pad((w[:, np.asarray(_slot_cols(_IW))] * iw_scale).T, ((0, GT_ROWS - N_HEADS), (0, 0))).astype(BF16)
    return wq_t, wk, wv_t, wb, wg_t


def _cparams(sem):
    return pltpu.CompilerParams(dimension_semantics=sem, vmem_limit_bytes=VMEM_LIMIT)


def _dot(a, b):
    return jnp.dot(a, b, preferred_element_type=F32)


def _dot_nt(a, b):
    return lax.dot_general(a, b, (((1,), (1,)), ((), ())), preferred_element_type=F32)


def _split3(a):
    a1 = a.astype(BF16)
    r1 = a - a1.astype(F32)
    a2 = r1.astype(BF16)
    a3 = (r1 - a2.astype(F32)).astype(BF16)
    return a1, a2, a3


def _layernorm(z, g, b):
    mu = jnp.mean(z, axis=-1, keepdims=True)
    zc = z - mu
    var = jnp.mean(zc * zc, axis=-1, keepdims=True)
    return zc * lax.rsqrt(var + LN_EPS) * g + b


def _proj_kernel(x_ref, wq_ref, wk_ref, wv_ref, wb_ref, wg_ref, qt_ref, ka_ref, vt_ref, pb_ref, gt_ref):
    xb = x_ref[...].astype(BF16)
    qt_ref[...] = _dot_nt(wq_ref[...], xb).astype(qt_ref.dtype)
    ka_ref[...] = _dot(xb, wk_ref[...]).astype(ka_ref.dtype)
    vt = _dot_nt(wv_ref[...], xb)
    for g in range(3):
        vt_ref[g, 0] = vt[g * GROUP:(g + 1) * GROUP, :].astype(vt_ref.dtype)
    pb_ref[...] = _dot(xb, wb_ref[...])
    gt_ref[...] = _dot_nt(wg_ref[...], xb)


def _proj(x2d, weights, t):
    T = x2d.shape[0]
    full = lambda a: pl.BlockSpec(a.shape, lambda i: (0,) * a.ndim)
    return pl.pallas_call(
        _proj_kernel,
        grid=(T // t,),
        in_specs=[pl.BlockSpec((t, D_MODEL), lambda i: (i, 0))] + [full(a) for a in weights],
        out_specs=[pl.BlockSpec((4 * GROUP, t), lambda i: (0, i)),
                   pl.BlockSpec((t, NK), lambda i: (i, 0)),
                   pl.BlockSpec((3, 1, GROUP, t), lambda i: (0, i, 0, 0)),
                   pl.BlockSpec((t, NB), lambda i: (i, 0)),
                   pl.BlockSpec((GT_ROWS, t), lambda i: (0, i))],
        out_shape=[jax.ShapeDtypeStruct((4 * GROUP, T), BF16),
                   jax.ShapeDtypeStruct((T, NK), BF16),
                   jax.ShapeDtypeStruct((3, T // t, GROUP, t), BF16),
                   jax.ShapeDtypeStruct((T, NB), F32),
                   jax.ShapeDtypeStruct((GT_ROWS, T), F32)],
        compiler_params=_cparams(("parallel",)),
        name="proj",
    )(x2d, *weights)


def _foxgate_kernel(g_ref, fb_ref, c_ref, *, chunk):
    L = g_ref.shape[0]
    row = lax.broadcasted_iota(I32, (chunk, chunk), 0)
    col = lax.broadcasted_iota(I32, (chunk, chunk), 1)
    tri = jnp.where(col <= row, 1.0, 0.0).astype(BF16)
    lane = lax.broadcasted_iota(I32, (chunk, LANE), 1)

    def body(i, carry):
        r0 = pl.multiple_of(i * chunk, chunk)
        z = g_ref[pl.ds(r0, chunk), :] + fb_ref[...]
        lf = jnp.minimum(z, 0.0) - jnp.log1p(jnp.exp(-jnp.abs(z)))
        p1, p2, p3 = _split3(lf)
        cs = _dot(tri, p1) + _dot(tri, p2) + _dot(tri, p3) + carry
        cs = jnp.where(lane < N_HEADS, cs, 0.0)
        c1 = cs.astype(BF16).astype(F32)
        r1 = cs - c1
        c2 = r1.astype(BF16).astype(F32)
        c3 = r1 - c2
        out = c1 + pltpu.roll(c2, N_HEADS, axis=1) + pltpu.roll(c3, 2 * N_HEADS, axis=1)
        c_ref[pl.ds(r0, chunk), :] = out.astype(c_ref.dtype)
        return cs[chunk - 1:chunk, :]

    lax.fori_loop(0, L // chunk, body, jnp.zeros((1, LANE), F32))


def _foxgate(pb, fb_row, B, L):
    return pl.pallas_call(
        functools.partial(_foxgate_kernel, chunk=256),
        grid=(B,),
        in_specs=[pl.BlockSpec((L, LANE), lambda b: (b, GATE_BLOCK)),
                  pl.BlockSpec((1, LANE), lambda b: (0, 0))],
        out_specs=pl.BlockSpec((L, LANE), lambda b: (b, 0)),
        out_shape=jax.ShapeDtypeStruct((B * L, LANE), BF16),
        compiler_params=_cparams(("parallel",)),
        name="foxgate",
    )(pb, fb_row)


def _online_update(carry, s, vt):
    m, l, acc = carry
    m_new = jnp.maximum(m, jnp.max(s, axis=0, keepdims=True))
    alpha = jnp.exp(m - m_new)
    p = jnp.exp(s - m_new)
    l = alpha * l + jnp.sum(p, axis=0, keepdims=True)
    acc = alpha * acc + _dot(vt, p.astype(BF16))
    return m_new, l, acc


def _softmax_init(t):
    return (jnp.full((1, t), NEG, F32), jnp.zeros((1, t), F32), jnp.zeros((HEAD_DIM, t), F32))


def _causal(t):
    return lax.broadcasted_iota(I32, (t, t), 0) <= lax.broadcasted_iota(I32, (t, t), 1)


def _pair_rows(q_ref, h):
    p = h // 2
    return q_ref[p * LANE:(p + 1) * LANE, :]


def _q_spec(L, t, group):
    nq = L // t
    return pl.BlockSpec((GROUP, t), lambda b, i: (group, b * nq + i))


def _k_spec(L, block):
    return pl.BlockSpec((L, GROUP), lambda b, i: (b, block))


def _v_spec(L, t, group):
    return pl.BlockSpec((1, L // t, GROUP, t), lambda b, i: (group, b, 0, 0))


def _out_spec(L, t):
    nq = L // t
    return pl.BlockSpec((t, GROUP), lambda b, i: (b * nq + i, 0))


def _fox_kernel(q_ref, k_ref, ca_ref, v_ref, o_ref, qa_ref, *, t):
    i = pl.program_id(1)
    row = lax.broadcasted_iota(I32, (LANE, t), 0)
    for h in range(N_HEADS):
        qz = jnp.where((row // HEAD_DIM) == (h % 2), _pair_rows(q_ref, h), 0)
        sel = jnp.where((row == h) | (row == N_HEADS + h) | (row == 2 * N_HEADS + h), -1.0, 0.0).astype(BF16)
        qa_ref[h // 2, :, (h % 2) * t:(h % 2 + 1) * t] = jnp.concatenate([qz.astype(BF16), sel], axis=0)
    keep = _causal(t)

    def step(j, carry, masked):
        k0 = pl.multiple_of(j * t, t)
        kk = k_ref[pl.ds(k0, t), :]
        ca = ca_ref[pl.ds(k0, t), :]
        vt = v_ref[0, j]
        sp = [_dot(jnp.concatenate([kk[:, p * LANE:(p + 1) * LANE], ca], axis=1), qa_ref[p])
              for p in range(N_HEADS // 2)]
        out = []
        for h in range(N_HEADS):
            s = sp[h // 2][:, (h % 2) * t:(h % 2 + 1) * t]
            if masked:
                s = jnp.where(keep, s, NEG)
            out.append(_online_update(carry[h], s, vt[h * HEAD_DIM:(h + 1) * HEAD_DIM, :]))
        return tuple(out)

    carry = lax.fori_loop(0, i, functools.partial(step, masked=False),
                          tuple(_softmax_init(t) for _ in range(N_HEADS)))
    carry = step(i, carry, True)
    ot = jnp.concatenate([acc / l for (_, l, acc) in carry], axis=0)
    o_ref[...] = ot.T.astype(o_ref.dtype)


def _fox(qt, ka, caug, vt, B, L, t):
    return pl.pallas_call(
        functools.partial(_fox_kernel, t=t),
        grid=(B, L // t),
        in_specs=[_q_spec(L, t, QT_GROUP["f"]), _k_spec(L, KA_BLOCK["f"]),
                  pl.BlockSpec((L, LANE), lambda b, i: (b, 0)), _v_spec(L, t, 0)],
        out_specs=_out_spec(L, t),
        out_shape=jax.ShapeDtypeStruct((B * L, GROUP), BF16),
        scratch_shapes=[pltpu.VMEM((N_HEADS // 2, 2 * LANE, 2 * t), BF16)],
        compiler_params=_cparams(("parallel", "arbitrary")),
        name="fox",
    )(qt, ka, caug, vt)


def _diff_kernel(lam_ref, q_ref, k_ref, v_ref, dtab_ref, g_ref, o_ref, qz_ref, *, t, out_scale):
    i = pl.program_id(1)
    lam = lam_ref[0]
    scale = DIFF_QK ** -0.5
    row = lax.broadcasted_iota(I32, (LANE, t), 0)
    for h in range(N_HEADS):
        for m in range(2):
            lo = (h % 2) * HEAD_DIM + m * DIFF_QK
            n = 2 * (h % 2) + m
            qz_ref[h // 2, :, n * t:(n + 1) * t] = jnp.where((row >= lo) & (row < lo + DIFF_QK),
                                                            _pair_rows(q_ref, h), 0).astype(BF16)

    pieces = []
    for p in range(N_HEADS // 2):
        def step(j, carry, near, p=p):
            k0 = pl.multiple_of(j * t, t)
            vt = v_ref[0, j]
            sp = _dot(k_ref[pl.ds(k0, t), p * LANE:(p + 1) * LANE], qz_ref[p])
            out = []
            for hh in range(2):
                h = 2 * p + hh
                for m in range(2):
                    n = 2 * hh + m
                    s = sp[:, n * t:(n + 1) * t] * scale
                    if near:
                        s = s + dtab_ref[h, i - j]
                    out.append(_online_update(carry[2 * hh + m], s, vt[h * HEAD_DIM:(h + 1) * HEAD_DIM, :]))
            return tuple(out)

        n_far = jnp.maximum(i - 1, 0)
        carry = lax.fori_loop(0, n_far, functools.partial(step, near=False),
                              tuple(_softmax_init(t) for _ in range(4)))
        carry = lax.fori_loop(n_far, i + 1, functools.partial(step, near=True), carry)
        for hh in range(2):
            (_, l1, a1), (_, l2, a2) = carry[2 * hh], carry[2 * hh + 1]
            o = a1 / l1 - lam * (a2 / l2)
            o = o * lax.rsqrt(jnp.mean(o * o, axis=0, keepdims=True) + LN_EPS)
            pieces.append(o * g_ref[...] * out_scale)
    o_ref[...] = jnp.concatenate(pieces, axis=0).T.astype(o_ref.dtype)


def _diff(lam, qt, ka, vt, dtab, g_tile, B, L, t, out_scale):
    return pl.pallas_call(
        functools.partial(_diff_kernel, t=t, out_scale=out_scale),
        grid=(B, L // t),
        in_specs=[pl.BlockSpec(memory_space=pltpu.SMEM),
                  _q_spec(L, t, QT_GROUP["d"]), _k_spec(L, KA_BLOCK["d"]), _v_spec(L, t, 1),
                  pl.BlockSpec((N_HEADS, 2, t, t), lambda b, i: (0, 0, 0, 0)),
                  pl.BlockSpec((HEAD_DIM, t), lambda b, i: (0, 0))],
        out_specs=_out_spec(L, t),
        out_shape=jax.ShapeDtypeStruct((B * L, GROUP), BF16),
        scratch_shapes=[pltpu.VMEM((N_HEADS // 2, LANE, 4 * t), BF16)],
        compiler_params=_cparams(("parallel", "arbitrary")),
        name="diff",
    )(lam, qt, ka, vt, dtab, g_tile)


def _floor_avg(a, b):
    return (a >> 1) + (b >> 1) + (a & b & 1)


def _dsa_kernel(q_ref, k_ref, v_ref, iq_ref, ik_ref, iw_ref, dtab_ref, o_ref, keys_ref, qz_ref, iqz_ref,
                *, t, top_k, L):
    i = pl.program_id(1)
    nblk = i + 1
    keep = _causal(t)
    key_row = lax.broadcasted_iota(I32, (t, t), 0)
    q_pos = i * t + lax.broadcasted_iota(I32, (1, t), 1)
    row = lax.broadcasted_iota(I32, (LANE, t), 0)
    for h in range(N_HEADS):
        qz_ref[h // 2, :, (h % 2) * t:(h % 2 + 1) * t] = jnp.where(
            (row // HEAD_DIM) == (h % 2), _pair_rows(q_ref, h), 0).astype(BF16)
        iqz_ref[:, h * t:(h + 1) * t] = jnp.concatenate(
            [iq_ref[h * D_IDX:(h + 1) * D_IDX, :], jnp.zeros((LANE - D_IDX, t), BF16)], axis=0)

    iw = iw_ref[...]

    def score_block(c, _):
        k0 = pl.multiple_of(c * t, t)
        a = _dot(ik_ref[pl.ds(k0, t), :], iqz_ref[...])
        acc = None
        for h in range(N_HEADS):
            term = jnp.maximum(a[:, h * t:(h + 1) * t], 0.0) * iw[h:h + 1, :]
            acc = term if acc is None else acc + term
        bits = lax.bitcast_convert_type(acc, I32)
        key = jnp.where(bits < 0, bits ^ INT_MAX, bits)
        keys_ref[c] = key - (key >> 31)
        return 0

    lax.fori_loop(0, nblk, score_block, 0)
    keys_ref[i] = jnp.where(keep, keys_ref[i], INT_MIN)

    def count(pred):
        def body(c, acc):
            return acc + jnp.sum(pred(keys_ref[c], c).reshape(t // 8, 8, t), axis=0)
        acc = lax.fori_loop(0, nblk, body, jnp.zeros((8, t), F32))
        return jnp.sum(acc, axis=0, keepdims=True).astype(I32)

    n_valid = q_pos + 1
    small = n_valid <= top_k
    lo0 = jnp.full((1, t), INT_MIN + 1, I32)
    state0 = dict(lo=lo0, hi=jnp.full((1, t), INT_MAX, I32), clo=n_valid, chi=jnp.zeros((1, t), I32),
                  thr=lo0, need=jnp.zeros((1, t), I32), tied=jnp.zeros((1, t), I32), done=small.astype(I32))

    def n_active(done):
        return jnp.sum((1 - done).astype(F32))

    def body(c):
        _, st = c
        lo, hi, clo, chi, done = st["lo"], st["hi"], st["clo"], st["chi"], st["done"]
        mid = _floor_avg(lo, hi)
        mid = jnp.where((lo == 0) & (hi == INT_MAX), 1, mid)
        cnt = count(lambda kc, _c: jnp.where(kc >= mid, 1.0, 0.0))
        act = done == 0
        ge = cnt >= top_k
        lo_n = jnp.where(ge, mid, lo)
        clo_n = jnp.where(ge, cnt, clo)
        hi_n = jnp.where(ge, hi, mid)
        chi_n = jnp.where(ge, chi, cnt)
        exact = cnt == top_k
        fin = act & (exact | (hi_n == lo_n + 1))
        upd = lambda new, old: jnp.where(act, new, old)
        st_n = dict(lo=upd(lo_n, lo), hi=upd(hi_n, hi), clo=upd(clo_n, clo), chi=upd(chi_n, chi),
                    thr=jnp.where(fin, jnp.where(exact, mid, lo_n), st["thr"]),
                    need=jnp.where(fin, top_k - chi_n, st["need"]),
                    tied=jnp.where(fin, ((~exact) & (clo_n > top_k)).astype(I32), st["tied"]),
                    done=jnp.where(fin, 1, done))
        return n_active(st_n["done"]), st_n

    _, st = lax.while_loop(lambda c: c[0] > 0.5, body, (n_active(state0["done"]), state0))
    thr, need, tied = st["thr"], st["need"], st["tied"] > 0

    def tie_cut(_):
        def pbody(_it, c):
            plo, phi = c
            pmid = (plo + phi) >> 1
            cnt = count(lambda kc, cc: jnp.where((kc == thr) & (key_row + cc * t <= pmid), 1.0, 0.0))
            ok = cnt >= need
            return jnp.where(ok, plo, pmid), jnp.where(ok, pmid, phi)
        n_it = int(math.ceil(math.log2(L))) + 1
        _, phi = lax.fori_loop(0, n_it, pbody, (jnp.full((1, t), -1, I32), jnp.full((1, t), L - 1, I32)))
        return jnp.where(tied, phi, L)

    jcut = lax.cond(jnp.sum(tied.astype(F32)) > 0.5, tie_cut, lambda _: jnp.full((1, t), L, I32), 0)

    def attn_block(c, carry, near):
        k0 = pl.multiple_of(c * t, t)
        kk = k_ref[pl.ds(k0, t), :]
        vt = v_ref[0, c]
        sel = (keys_ref[c] + jnp.where(key_row + c * t <= jcut, 1, 0)) > thr
        sp = [_dot(kk[:, p * LANE:(p + 1) * LANE], qz_ref[p]) for p in range(N_HEADS // 2)]
        out = []
        for h in range(N_HEADS):
            s = sp[h // 2][:, (h % 2) * t:(h % 2 + 1) * t]
            if near:
                s = s + dtab_ref[h, i - c]
            s = jnp.where(sel, s, NEG)
            out.append(_online_update(carry[h], s, vt[h * HEAD_DIM:(h + 1) * HEAD_DIM, :]))
        return tuple(out)

    n_far = jnp.maximum(i - 1, 0)
    carry = lax.fori_loop(0, n_far, functools.partial(attn_block, near=False),
                          tuple(_softmax_init(t) for _ in range(N_HEADS)))
    carry = lax.fori_loop(n_far, nblk, functools.partial(attn_block, near=True), carry)
    ot = jnp.concatenate([acc / l for (_, l, acc) in carry], axis=0)
    o_ref[...] = ot.T.astype(o_ref.dtype)


def _dsa(qt, ka, vt, gt, dtab, B, L, t, top_k):
    nq = L // t
    return pl.pallas_call(
        functools.partial(_dsa_kernel, t=t, top_k=top_k, L=L),
        grid=(B, nq),
        in_specs=[_q_spec(L, t, QT_GROUP["s"]), _k_spec(L, KA_BLOCK["s"]), _v_spec(L, t, 2),
                  _q_spec(L, t, QT_GROUP["i"]),
                  pl.BlockSpec((L, LANE), lambda b, i: (b, IK_BLOCK)),
                  pl.BlockSpec((GT_ROWS, t), lambda b, i: (0, b * nq + i)),
                  pl.BlockSpec((N_HEADS, 2, t, t), lambda b, i: (0, 0, 0, 0))],
        out_specs=_out_spec(L, t),
        out_shape=jax.ShapeDtypeStruct((B * L, GROUP), BF16),
        scratch_shapes=[pltpu.VMEM((nq, t, t), I32), pltpu.VMEM((N_HEADS // 2, LANE, 2 * t), BF16),
                        pltpu.VMEM((LANE, N_HEADS * t), BF16)],
        compiler_params=_cparams(("parallel", "arbitrary")),
        name="dsa",
    )(qt, ka, vt, qt, ka, gt, dtab)


def _conv_kernel(u_ref, halo_ref, w_ref, cb_ref, g_ref, b_ref, o_ref, h_ref, *, tc):
    i = pl.program_id(1)

    def glu(u):
        return u[:, :GROUP] * jax.nn.sigmoid(u[:, GROUP:])

    h_ref[pl.ds(CONV_HALO, tc), :] = glu(u_ref[...])
    h_ref[pl.ds(0, CONV_HALO), :] = jnp.where(i > 0, glu(halo_ref[...]), 0.0)
    acc = jnp.zeros((tc, GROUP), F32)
    for j in range(CONV_W):
        acc = acc + h_ref[pl.ds(CONV_HALO - (CONV_W - 1) + j, tc), :] * w_ref[j:j + 1, :]
    y = _layernorm(acc + cb_ref[...], g_ref[...], b_ref[...])
    o_ref[...] = (y * jax.nn.sigmoid(y)).astype(o_ref.dtype)


def _conv(pb, conv_w, cb, g, b, B, L, tc):
    nc = L // tc
    r = tc // CONV_HALO
    row = lambda a: a.reshape(1, GROUP)
    return pl.pallas_call(
        functools.partial(_conv_kernel, tc=tc),
        grid=(B, nc),
        in_specs=[pl.BlockSpec((tc, 2 * GROUP), lambda bb, i: (bb * nc + i, 0)),
                  pl.BlockSpec((CONV_HALO, 2 * GROUP), lambda bb, i: (jnp.maximum((bb * nc + i) * r - 1, 0), 0)),
                  pl.BlockSpec((CONV_W, GROUP), lambda bb, i: (0, 0)),
                  pl.BlockSpec((1, GROUP), lambda bb, i: (0, 0)),
                  pl.BlockSpec((1, GROUP), lambda bb, i: (0, 0)),
                  pl.BlockSpec((1, GROUP), lambda bb, i: (0, 0))],
        out_specs=pl.BlockSpec((tc, GROUP), lambda bb, i: (bb * nc + i, 0)),
        out_shape=jax.ShapeDtypeStruct((B * L, GROUP), BF16),
        scratch_shapes=[pltpu.VMEM((tc + CONV_HALO, GROUP), F32)],
        compiler_params=_cparams(("parallel", "arbitrary")),
        name="conv",
    )(pb, pb, conv_w, row(cb), row(g), row(b))


def _mix_kernel(yf_ref, yd_ref, ys_ref, yc_ref, x_ref, wo_ref, g_ref, b_ref, rw_ref, rb_ref,
                x1_ref, idx_ref, wt_ref, *, alpha):
    mix = None
    for n, y_ref in enumerate((yf_ref, yd_ref, ys_ref, yc_ref)):
        part = _dot(y_ref[...], wo_ref[n * GROUP:(n + 1) * GROUP, :])
        mix = part if mix is None else mix + part
    x1 = _layernorm(alpha * x_ref[...] + mix, g_ref[...], b_ref[...])
    x1_ref[...] = x1

    a1, a2, a3 = _split3(x1)
    w1, w2, w3 = rw_ref[0], rw_ref[1], rw_ref[2]
    lg = (_dot(a1, w1) + (_dot(a1, w2) + _dot(a2, w1)) + (_dot(a1, w3) + _dot(a2, w2) + _dot(a3, w1))) + rb_ref[...]

    tm = lg.shape[0]
    lane = lax.broadcasted_iota(I32, (tm, LANE), 1)
    lane_f = lane.astype(F32)
    idx_out = jnp.zeros((tm, LANE), I32)
    vals = []
    for r in range(TOP_K):
        mx = jnp.max(lg, axis=1, keepdims=True)
        am = jnp.min(jnp.where(lg == mx, lane_f, float(LANE)), axis=1, keepdims=True).astype(I32)
        vals.append(mx)
        idx_out = jnp.where(lane == r, am, idx_out)
        lg = jnp.where(lane == am, 2 * NEG, lg)
    es = [jnp.exp(v - vals[0]) for v in vals]
    den = es[0] + es[1] + es[2] + es[3]
    wt = jnp.zeros((tm, LANE), F32)
    for r in range(TOP_K):
        wt = jnp.where(lane == r, es[r] / den, wt)
    idx_ref[...] = idx_out
    wt_ref[...] = wt


def _mix(ys, x2d, wo, g, b, rw3, rb, alpha, tm):
    T = x2d.shape[0]
    yspec = pl.BlockSpec((tm, GROUP), lambda i: (i, 0))
    full = lambda shp: pl.BlockSpec(shp, lambda i: (0,) * len(shp))
    return pl.pallas_call(
        functools.partial(_mix_kernel, alpha=alpha),
        grid=(T // tm,),
        in_specs=[yspec, yspec, yspec, yspec,
                  pl.BlockSpec((tm, D_MODEL), lambda i: (i, 0)),
                  full((D_MODEL, D_MODEL)), full((1, D_MODEL)), full((1, D_MODEL)),
                  full((3, D_MODEL, LANE)), full((1, LANE))],
        out_specs=[pl.BlockSpec((tm, D_MODEL), lambda i: (i, 0)),
                   pl.BlockSpec((tm, LANE), lambda i: (i, 0)),
                   pl.BlockSpec((tm, LANE), lambda i: (i, 0))],
        out_shape=[jax.ShapeDtypeStruct((T, D_MODEL), F32),
                   jax.ShapeDtypeStruct((T, LANE), I32),
                   jax.ShapeDtypeStruct((T, LANE), F32)],
        compiler_params=_cparams(("parallel",)),
        name="mix",
    )(*ys, x2d, wo, g.reshape(1, D_MODEL), b.reshape(1, D_MODEL), rw3, rb)


def _start_row_gather(idx_ref, base, n, src_hbm, dst_ref, sem):
    def body(r, _):
        pltpu.make_async_copy(src_hbm.at[pl.ds(idx_ref[base + r], 1)], dst_ref.at[pl.ds(r, 1)], sem).start()
        return 0
    lax.fori_loop(0, n, body, 0, unroll=8)


def _wait_row_gather(n, src_hbm, dst_ref, sem):
    pltpu.make_async_copy(src_hbm.at[pl.ds(0, n)], dst_ref, sem).wait()


def _expert_kernel(te_ref, tv_ref, src_ref, x_hbm, wgu_ref, bgu_ref, wd_ref, bd_ref, y_ref, xbuf, sem, *, tmm):
    i = pl.program_id(0)
    n = pl.num_programs(0)
    slot = lax.rem(i, 2)

    @pl.when(i == 0)
    def _():
        _start_row_gather(src_ref, 0, tmm, x_hbm, xbuf.at[0], sem.at[0])

    @pl.when(i + 1 < n)
    def _():
        _start_row_gather(src_ref, (i + 1) * tmm, tmm, x_hbm, xbuf.at[1 - slot], sem.at[1 - slot])

    _wait_row_gather(tmm, x_hbm, xbuf.at[slot], sem.at[slot])

    @pl.when(tv_ref[i] > 0)
    def _():
        xb = xbuf[slot].astype(BF16)
        gu = _dot(xb, wgu_ref[0]) + bgu_ref[0]
        gate = jnp.minimum(gu, SWIGLU_LIMIT)
        up = jnp.clip(pltpu.roll(gu, gu.shape[1] - 1, axis=1), -SWIGLU_LIMIT, SWIGLU_LIMIT)
        h = (up + 1.0) * gate * jax.nn.sigmoid(gate * SWIGLU_ALPHA)
        y_ref[...] = _dot(h.astype(BF16), wd_ref[0]) + bd_ref[0]

    @pl.when(tv_ref[i] == 0)
    def _():
        y_ref[...] = jnp.zeros_like(y_ref)


def _experts(tile_e, tile_v, src, x1, wgu, bgu, wd2, bd, R, tmm):
    return pl.pallas_call(
        functools.partial(_expert_kernel, tmm=tmm),
        grid_spec=pltpu.PrefetchScalarGridSpec(
            num_scalar_prefetch=3,
            grid=(R // tmm,),
            in_specs=[pl.BlockSpec(memory_space=pl.ANY),
                      pl.BlockSpec((1, D_MODEL, 2 * D_MODEL), lambda i, te, tv, s: (te[i], 0, 0)),
                      pl.BlockSpec((1, 1, 2 * D_MODEL), lambda i, te, tv, s: (te[i], 0, 0)),
                      pl.BlockSpec((1, 2 * D_MODEL, D_MODEL), lambda i, te, tv, s: (te[i], 0, 0)),
                      pl.BlockSpec((1, 1, D_MODEL), lambda i, te, tv, s: (te[i], 0, 0))],
            out_specs=pl.BlockSpec((tmm, D_MODEL), lambda i, te, tv, s: (i, 0)),
            scratch_shapes=[pltpu.VMEM((2, tmm, D_MODEL), F32), pltpu.SemaphoreType.DMA((2,))],
        ),
        out_shape=jax.ShapeDtypeStruct((R, D_MODEL), F32),
        compiler_params=_cparams(("arbitrary",)),
        name="moe_experts",
    )(tile_e, tile_v, src, x1, wgu, bgu, wd2, bd)


def _combine_kernel(pos_ref, ys_hbm, x1_ref, wt_ref, g_ref, b_ref, o_ref, buf, sem, *, tc, alpha):
    i = pl.program_id(0)
    n = pl.num_programs(0)
    slot = lax.rem(i, 2)
    rows = tc * TOP_K

    @pl.when(i == 0)
    def _():
        _start_row_gather(pos_ref, 0, rows, ys_hbm, buf.at[0], sem.at[0])

    @pl.when(i + 1 < n)
    def _():
        _start_row_gather(pos_ref, (i + 1) * rows, rows, ys_hbm, buf.at[1 - slot], sem.at[1 - slot])

    _wait_row_gather(rows, ys_hbm, buf.at[slot], sem.at[slot])
    wt = wt_ref[...]
    ffn = buf[slot, pl.ds(0, tc), :] * wt[:, 0:1]
    for k in range(1, TOP_K):
        ffn = ffn + buf[slot, pl.ds(k * tc, tc), :] * wt[:, k:k + 1]
    o_ref[...] = _layernorm(alpha * x1_ref[...] + ffn, g_ref[...], b_ref[...])


def _combine(pos, ys, x1, wt, g, b, alpha, tc):
    T = x1.shape[0]
    pos_t = jnp.transpose(pos.reshape(T // tc, tc, TOP_K), (0, 2, 1)).reshape(T * TOP_K)
    return pl.pallas_call(
        functools.partial(_combine_kernel, tc=tc, alpha=alpha),
        grid_spec=pltpu.PrefetchScalarGridSpec(
            num_scalar_prefetch=1,
            grid=(T // tc,),
            in_specs=[pl.BlockSpec(memory_space=pl.ANY),
                      pl.BlockSpec((tc, D_MODEL), lambda i, p: (i, 0)),
                      pl.BlockSpec((tc, LANE), lambda i, p: (i, 0)),
                      pl.BlockSpec((1, D_MODEL), lambda i, p: (0, 0)),
                      pl.BlockSpec((1, D_MODEL), lambda i, p: (0, 0))],
            out_specs=pl.BlockSpec((tc, D_MODEL), lambda i, p: (i, 0)),
            scratch_shapes=[pltpu.VMEM((2, TOP_K * tc, D_MODEL), F32), pltpu.SemaphoreType.DMA((2,))],
        ),
        out_shape=jax.ShapeDtypeStruct((T, D_MODEL), F32),
        compiler_params=_cparams(("arbitrary",)),
        name="moe_combine",
    )(pos_t, ys, x1, wt, g.reshape(1, D_MODEL), b.reshape(1, D_MODEL))


def _moe_plan(top_idx, tmm):
    T = top_idx.shape[0]
    F = T * TOP_K
    R = F + N_EXPERTS * tmm
    flat_e = top_idx.reshape(F)
    onehot = (flat_e[:, None] == jnp.arange(N_EXPERTS, dtype=I32)[None, :]).astype(I32)
    csum = jnp.cumsum(onehot, axis=0)
    counts = csum[-1]
    rank_in_e = jnp.take_along_axis(csum, flat_e[:, None], axis=1)[:, 0] - 1
    padded = ((counts + tmm - 1) // tmm) * tmm
    gend = jnp.cumsum(padded)
    gstart = gend - padded
    ustart = jnp.cumsum(counts) - counts
    pos = gstart[flat_e] + rank_in_e
    order = jnp.argsort(flat_e, stable=True).astype(I32)
    rows = jnp.arange(R, dtype=I32)
    e_row = jnp.minimum(jnp.searchsorted(gend, rows, side="right").astype(I32), N_EXPERTS - 1)
    off = rows - gstart[e_row]
    valid = off < counts[e_row]
    src = jnp.where(valid, order[jnp.clip(ustart[e_row] + off, 0, F - 1)] // TOP_K, 0)
    tile_rows = rows[::tmm]
    tile_v = (tile_rows < gend[-1]).astype(I32)
    tile_e = jnp.where(tile_v > 0, e_row[::tmm], e_row[jnp.maximum(gend[-1] - 1, 0)])
    return pos.astype(I32), src.astype(I32), tile_e.astype(I32), tile_v, R


def _rel_bucket(dist):
    n = jnp.maximum(dist, 0)
    max_exact = NUM_BUCKETS // 2
    nf = jnp.maximum(n, 1).astype(F32)
    large = max_exact + (jnp.log(nf / max_exact) / math.log(MAX_DISTANCE / max_exact)
                         * (NUM_BUCKETS - max_exact)).astype(I32)
    large = jnp.minimum(large, NUM_BUCKETS - 1)
    return jnp.where(n < max_exact, n, large)


def _bias_tables(table, t):
    assert t >= MAX_DISTANCE, "key blocks two or more away must lie entirely in the last bucket"
    d0 = jnp.arange(t, dtype=I32)[None, :] - jnp.arange(t, dtype=I32)[:, None]
    rel = table - table[NUM_BUCKETS - 1][None, :]
    out = []
    for d in (d0, d0 + t):
        b = jnp.transpose(rel[_rel_bucket(d)], (2, 0, 1))
        out.append(jnp.where((d >= 0)[None], b, NEG))
    return jnp.stack(out, axis=1).astype(F32)


def kernel(x, rel_bias, w_in, forget_b, diff_lambda, diff_norm_g, conv_w, conv_b, conv_ln_g, conv_ln_b,
           w_out, ln1_g, ln1_b, router_w, router_b, w_gu, b_gu, w_down, b_down, ln2_g, ln2_b):
    B, L, D = x.shape
    depth = w_in.shape[0]
    assert D == D_MODEL
    T = B * L
    t = 256
    tm = 256
    tmm = 256
    tcv = 512 if L % 512 == 0 else 256
    assert L % t == 0 and T % tm == 0
    top_k = min(K_SEL_MAX, L // 4)
    alpha = (2 * depth) ** 0.25

    dtab_diff = _bias_tables(rel_bias[:, :N_HEADS], t)
    dtab_dsa = _bias_tables(rel_bias[:, N_HEADS:], t)

    x2d = x.reshape(T, D)
    for l in range(depth):
        lambda_init = 0.8 - 0.6 * math.exp(-0.3 * l)
        qt, ka, vt, pb, gt = _proj(x2d, _proj_weights(w_in[l]), t)

        fb_row = jnp.zeros((1, LANE), F32).at[0, :N_HEADS].set(forget_b[l])
        caug = _foxgate(pb, fb_row, B, L)
        y_fox = _fox(qt, ka, caug, vt, B, L, t)

        lp = diff_lambda[l].astype(F32)
        lam = (jnp.exp(jnp.sum(lp[0] * lp[1])) - jnp.exp(jnp.sum(lp[2] * lp[3])) + lambda_init).reshape(1)
        g_tile = jnp.broadcast_to(diff_norm_g[l].astype(F32)[:, None], (HEAD_DIM, t))
        y_diff = _diff(lam, qt, ka, vt, dtab_diff, g_tile, B, L, t, 1.0 - lambda_init)

        y_dsa = _dsa(qt, ka, vt, gt, dtab_dsa, B, L, t, top_k)
        y_conv = _conv(pb, conv_w[l], conv_b[l], conv_ln_g[l], conv_ln_b[l], B, L, tcv)

        rw = jnp.zeros((D, LANE), F32).at[:, :N_EXPERTS].set(router_w[l])
        r1 = rw.astype(BF16)
        r2 = (rw - r1.astype(F32)).astype(BF16)
        r3 = (rw - r1.astype(F32) - r2.astype(F32)).astype(BF16)
        rb = jnp.full((1, LANE), NEG, F32).at[0, :N_EXPERTS].set(router_b[l])
        x1, top_idx, top_w = _mix((y_fox, y_diff, y_dsa, y_conv), x2d, w_out[l].astype(BF16),
                                  ln1_g[l], ln1_b[l], jnp.stack([r1, r2, r3]), rb, alpha, tm)

        pos, src, tile_e, tile_v, R = _moe_plan(top_idx[:, :TOP_K], tmm)
        wd = w_down[l].astype(BF16)
        wd2 = jnp.stack([wd, jnp.zeros_like(wd)], axis=2).reshape(N_EXPERTS, 2 * D, D)
        ys = _experts(tile_e, tile_v, src, x1, w_gu[l].astype(BF16), b_gu[l][:, None, :], wd2,
                      b_down[l][:, None, :], R, tmm)
        x2d = _combine(pos, ys, x1, top_w, ln2_g[l], ln2_b[l], alpha, 128)
    return x2d.reshape(B, L, D)
```

```python
import functools
import math

import numpy as np
import jax
import jax.numpy as jnp
from jax import lax
from jax.experimental import pallas as pl
from jax.experimental.pallas import tpu as pltpu

F32 = jnp.float32
BF16 = jnp.bfloat16
I32 = jnp.int32

D_MODEL = 1024
HEAD_DIM = 64
N_HEADS = 4
GROUP = N_HEADS * HEAD_DIM
DIFF_QK = HEAD_DIM // 2
D_IDX = 64
K_SEL_MAX = 256
CONV_W = 31
CONV_HALO = 32
NUM_BUCKETS = 32
MAX_DISTANCE = 128
N_EXPERTS = 32
TOP_K = 4
SWIGLU_LIMIT = 7.0
SWIGLU_ALPHA = 1.702
LN_EPS = 1e-5
NEG = -1e30
INT_MIN = -(2 ** 31)
INT_MAX = 2 ** 31 - 1
LANE = 128
VMEM_LIMIT = 56 * 1024 * 1024

_IN_WIDTHS = (GROUP, GROUP, GROUP, N_HEADS, GROUP, GROUP, GROUP, GROUP, GROUP, GROUP,
              N_HEADS * D_IDX, D_IDX, N_HEADS, 2 * GROUP)
_IN_OFF = tuple(int(v) for v in np.cumsum((0,) + _IN_WIDTHS))
(_FQ, _FK, _FV, _FF, _DQ, _DK, _DV, _SQ, _SK, _SV, _IQ, _IK, _IW, _CU) = range(14)

QT_GROUP = {"f": 0, "d": 1, "s": 2, "i": 3}
KA_BLOCK = {"f": 0, "d": 1, "s": 2}
NK = 3 * GROUP + LANE
IK_BLOCK = 3 * GROUP // LANE
NB = 2 * GROUP + LANE
GATE_BLOCK = 2 * GROUP // LANE
GT_ROWS = 16


def _cols(w, slots, scales=None):
    parts = []
    for n, s in enumerate(slots):
        part = w[:, _IN_OFF[s]:_IN_OFF[s + 1]]
        parts.append(part if scales is None or scales[n] == 1.0 else part * scales[n])
    return jnp.concatenate(parts, axis=1)


def _proj_weights(w):
    qs = HEAD_DIM ** -0.5
    iw_scale = N_HEADS ** -0.5
    wq_t = _cols(w, (_FQ, _DQ, _SQ, _IQ), (qs, 1.0, qs, D_IDX ** -0.5)).T.astype(BF16)
    wk = jnp.pad(_cols(w, (_FK, _DK, _SK, _IK)), ((0, 0), (0, LANE - D_IDX))).astype(BF16)
    wv_t = _cols(w, (_FV, _DV, _SV)).T.astype(BF16)
    wb = jnp.pad(_cols(w, (_CU, _FF, _IW), (1.0, 1.0, iw_scale)),
                 ((0, 0), (0, LANE - 2 * N_HEADS))).astype(BF16)
    wg_t = jnp.pad((w[:, _IN_OFF[_IW]:_IN_OFF[_IW + 1]] * iw_scale).T,
                   ((0, GT_ROWS - N_HEADS), (0, 0))).astype(BF16)
    return wq_t, wk, wv_t, wb, wg_t


def _cparams(sem):
    return pltpu.CompilerParams(dimension_semantics=sem, vmem_limit_bytes=VMEM_LIMIT)


def _dot(a, b):
    return jnp.dot(a, b, preferred_element_type=F32)


def _dot_nt(a, b):
    return lax.dot_general(a, b, (((1,), (1,)), ((), ())), preferred_element_type=F32)


def _split3(a):
    a1 = a.astype(BF16)
    r1 = a - a1.astype(F32)
    a2 = r1.astype(BF16)
    a3 = (r1 - a2.astype(F32)).astype(BF16)
    return a1, a2, a3


def _layernorm(z, g, b):
    mu = jnp.mean(z, axis=-1, keepdims=True)
    zc = z - mu
    var = jnp.mean(zc * zc, axis=-1, keepdims=True)
    return zc * lax.rsqrt(var + LN_EPS) * g + b


def _proj_kernel(x_ref, wq_ref, wk_ref, wv_ref, wb_ref, wg_ref, qt_ref, ka_ref, vt_ref, pb_ref, gt_ref):
    xb = x_ref[...].astype(BF16)
    qt_ref[...] = _dot_nt(wq_ref[...], xb).astype(qt_ref.dtype)
    ka_ref[...] = _dot(xb, wk_ref[...]).astype(ka_ref.dtype)
    vt = _dot_nt(wv_ref[...], xb)
    for g in range(3):
        vt_ref[g, 0] = vt[g * GROUP:(g + 1) * GROUP, :].astype(vt_ref.dtype)
    pb_ref[...] = _dot(xb, wb_ref[...])
    gt_ref[...] = _dot_nt(wg_ref[...], xb)


def _proj(x2d, weights, t):
    T = x2d.shape[0]
    full = lambda a: pl.BlockSpec(a.shape, lambda i: (0,) * a.ndim)
    return pl.pallas_call(
        _proj_kernel,
        grid=(T // t,),
        in_specs=[pl.BlockSpec((t, D_MODEL), lambda i: (i, 0))] + [full(a) for a in weights],
        out_specs=[pl.BlockSpec((4 * GROUP, t), lambda i: (0, i)),
                   pl.BlockSpec((t, NK), lambda i: (i, 0)),
                   pl.BlockSpec((3, 1, GROUP, t), lambda i: (0, i, 0, 0)),
                   pl.BlockSpec((t, NB), lambda i: (i, 0)),
                   pl.BlockSpec((GT_ROWS, t), lambda i: (0, i))],
        out_shape=[jax.ShapeDtypeStruct((4 * GROUP, T), BF16),
                   jax.ShapeDtypeStruct((T, NK), BF16),
                   jax.ShapeDtypeStruct((3, T // t, GROUP, t), BF16),
                   jax.ShapeDtypeStruct((T, NB), F32),
                   jax.ShapeDtypeStruct((GT_ROWS, T), F32)],
        compiler_params=_cparams(("parallel",)),
        name="proj",
    )(x2d, *weights)


def _foxgate_kernel(g_ref, fb_ref, c_ref, *, chunk):
    L = g_ref.shape[0]
    row = lax.broadcasted_iota(I32, (chunk, chunk), 0)
    col = lax.broadcasted_iota(I32, (chunk, chunk), 1)
    tri = jnp.where(col <= row, 1.0, 0.0).astype(BF16)
    lane = lax.broadcasted_iota(I32, (chunk, LANE), 1)

    def body(i, carry):
        r0 = pl.multiple_of(i * chunk, chunk)
        z = g_ref[pl.ds(r0, chunk), :] + fb_ref[...]
        lf = jnp.minimum(z, 0.0) - jnp.log1p(jnp.exp(-jnp.abs(z)))
        p1, p2, p3 = _split3(lf)
        cs = _dot(tri, p1) + _dot(tri, p2) + _dot(tri, p3) + carry
        cs = jnp.where(lane < N_HEADS, cs, 0.0)
        c1 = cs.astype(BF16).astype(F32)
        r1 = cs - c1
        c2 = r1.astype(BF16).astype(F32)
        c3 = r1 - c2
        out = c1 + pltpu.roll(c2, N_HEADS, axis=1) + pltpu.roll(c3, 2 * N_HEADS, axis=1)
        c_ref[pl.ds(r0, chunk), :] = out.astype(c_ref.dtype)
        return cs[chunk - 1:chunk, :]

    lax.fori_loop(0, L // chunk, body, jnp.zeros((1, LANE), F32))


def _foxgate(pb, fb_row, B, L):
    return pl.pallas_call(
        functools.partial(_foxgate_kernel, chunk=256),
        grid=(B,),
        in_specs=[pl.BlockSpec((L, LANE), lambda b: (b, GATE_BLOCK)),
                  pl.BlockSpec((1, LANE), lambda b: (0, 0))],
        out_specs=pl.BlockSpec((L, LANE), lambda b: (b, 0)),
        out_shape=jax.ShapeDtypeStruct((B * L, LANE), BF16),
        compiler_params=_cparams(("parallel",)),
        name="foxgate",
    )(pb, fb_row)


def _online_update(carry, s, vt):
    m, l, acc = carry
    m_new = jnp.maximum(m, jnp.max(s, axis=0, keepdims=True))
    alpha = jnp.exp(m - m_new)
    p = jnp.exp(s - m_new)
    l = alpha * l + jnp.sum(p, axis=0, keepdims=True)
    acc = alpha * acc + _dot(vt, p.astype(BF16))
    return m_new, l, acc


def _softmax_init(t):
    return (jnp.full((1, t), NEG, F32), jnp.zeros((1, t), F32), jnp.zeros((HEAD_DIM, t), F32))


def _causal(t):
    return lax.broadcasted_iota(I32, (t, t), 0) <= lax.broadcasted_iota(I32, (t, t), 1)


def _pair_rows(q_ref, h):
    p = h // 2
    return q_ref[p * LANE:(p + 1) * LANE, :]


def _q_spec(L, t, group):
    nq = L // t
    return pl.BlockSpec((GROUP, t), lambda b, i: (group, b * nq + i))


def _k_spec(L, block):
    return pl.BlockSpec((L, GROUP), lambda b, i: (b, block))


def _v_spec(L, t, group):
    return pl.BlockSpec((1, L // t, GROUP, t), lambda b, i: (group, b, 0, 0))


def _out_spec(L, t):
    nq = L // t
    return pl.BlockSpec((t, GROUP), lambda b, i: (b * nq + i, 0))


def _fox_kernel(q_ref, k_ref, ca_ref, v_ref, o_ref, qa_ref, *, t):
    i = pl.program_id(1)
    row = lax.broadcasted_iota(I32, (LANE, t), 0)
    for h in range(N_HEADS):
        qz = jnp.where((row // HEAD_DIM) == (h % 2), _pair_rows(q_ref, h), 0)
        sel = jnp.where((row == h) | (row == N_HEADS + h) | (row == 2 * N_HEADS + h), -1.0, 0.0).astype(BF16)
        qa_ref[h // 2, :, (h % 2) * t:(h % 2 + 1) * t] = jnp.concatenate([qz.astype(BF16), sel], axis=0)
    keep = _causal(t)

    def step(j, carry, masked):
        k0 = pl.multiple_of(j * t, t)
        kk = k_ref[pl.ds(k0, t), :]
        ca = ca_ref[pl.ds(k0, t), :]
        vt = v_ref[0, j]
        sp = [_dot(jnp.concatenate([kk[:, p * LANE:(p + 1) * LANE], ca], axis=1), qa_ref[p])
              for p in range(N_HEADS // 2)]
        out = []
        for h in range(N_HEADS):
            s = sp[h // 2][:, (h % 2) * t:(h % 2 + 1) * t]
            if masked:
                s = jnp.where(keep, s, NEG)
            out.append(_online_update(carry[h], s, vt[h * HEAD_DIM:(h + 1) * HEAD_DIM, :]))
        return tuple(out)

    carry = lax.fori_loop(0, i, functools.partial(step, masked=False),
                          tuple(_softmax_init(t) for _ in range(N_HEADS)))
    carry = step(i, carry, True)
    ot = jnp.concatenate([acc / l for (_, l, acc) in carry], axis=0)
    o_ref[...] = ot.T.astype(o_ref.dtype)


def _fox(qt, ka, caug, vt, B, L, t):
    return pl.pallas_call(
        functools.partial(_fox_kernel, t=t),
        grid=(B, L // t),
        in_specs=[_q_spec(L, t, QT_GROUP["f"]), _k_spec(L, KA_BLOCK["f"]),
                  pl.BlockSpec((L, LANE), lambda b, i: (b, 0)), _v_spec(L, t, 0)],
        out_specs=_out_spec(L, t),
        out_shape=jax.ShapeDtypeStruct((B * L, GROUP), BF16),
        scratch_shapes=[pltpu.VMEM((N_HEADS // 2, 2 * LANE, 2 * t), BF16)],
        compiler_params=_cparams(("parallel", "arbitrary")),
        name="fox",
    )(qt, ka, caug, vt)


def _diff_kernel(lam_ref, q_ref, k_ref, v_ref, dtab_ref, g_ref, o_ref, qz_ref, *, t, out_scale):
    i = pl.program_id(1)
    lam = lam_ref[0]
    scale = DIFF_QK ** -0.5
    row = lax.broadcasted_iota(I32, (LANE, t), 0)
    for h in range(N_HEADS):
        for m in range(2):
            lo = (h % 2) * HEAD_DIM + m * DIFF_QK
            n = 2 * (h % 2) + m
            qz_ref[h // 2, :, n * t:(n + 1) * t] = jnp.where((row >= lo) & (row < lo + DIFF_QK),
                                                            _pair_rows(q_ref, h), 0).astype(BF16)

    pieces = []
    for p in range(N_HEADS // 2):
        def step(j, carry, near, p=p):
            k0 = pl.multiple_of(j * t, t)
            vt = v_ref[0, j]
            sp = _dot(k_ref[pl.ds(k0, t), p * LANE:(p + 1) * LANE], qz_ref[p])
            out = []
            for hh in range(2):
                h = 2 * p + hh
                for m in range(2):
                    n = 2 * hh + m
                    s = sp[:, n * t:(n + 1) * t] * scale
                    if near:
                        s = s + dtab_ref[h, i - j]
                    out.append(_online_update(carry[2 * hh + m], s, vt[h * HEAD_DIM:(h + 1) * HEAD_DIM, :]))
            return tuple(out)

        n_far = jnp.maximum(i - 1, 0)
        carry = lax.fori_loop(0, n_far, functools.partial(step, near=False),
                              tuple(_softmax_init(t) for _ in range(4)))
        carry = lax.fori_loop(n_far, i + 1, functools.partial(step, near=True), carry)
        for hh in range(2):
            (_, l1, a1), (_, l2, a2) = carry[2 * hh], carry[2 * hh + 1]
            o = a1 / l1 - lam * (a2 / l2)
            o = o * lax.rsqrt(jnp.mean(o * o, axis=0, keepdims=True) + LN_EPS)
            pieces.append(o * g_ref[...] * out_scale)
    o_ref[...] = jnp.concatenate(pieces, axis=0).T.astype(o_ref.dtype)


def _diff(lam, qt, ka, vt, dtab, g_tile, B, L, t, out_scale):
    return pl.pallas_call(
        functools.partial(_diff_kernel, t=t, out_scale=out_scale),
        grid=(B, L // t),
        in_specs=[pl.BlockSpec(memory_space=pltpu.SMEM),
                  _q_spec(L, t, QT_GROUP["d"]), _k_spec(L, KA_BLOCK["d"]), _v_spec(L, t, 1),
                  pl.BlockSpec((N_HEADS, 2, t, t), lambda b, i: (0, 0, 0, 0)),
                  pl.BlockSpec((HEAD_DIM, t), lambda b, i: (0, 0))],
        out_specs=_out_spec(L, t),
        out_shape=jax.ShapeDtypeStruct((B * L, GROUP), BF16),
        scratch_shapes=[pltpu.VMEM((N_HEADS // 2, LANE, 4 * t), BF16)],
        compiler_params=_cparams(("parallel", "arbitrary")),
        name="diff",
    )(lam, qt, ka, vt, dtab, g_tile)


def _floor_avg(a, b):
    return (a >> 1) + (b >> 1) + (a & b & 1)


def _dsa_kernel(q_ref, k_ref, v_ref, iq_ref, ik_ref, iw_ref, dtab_ref, o_ref, keys_ref, qz_ref, iqz_ref, gm_ref,
                *, t, top_k, L):
    i = pl.program_id(1)
    nblk = i + 1
    keep = _causal(t)
    key_row = lax.broadcasted_iota(I32, (t, t), 0)
    q_pos = i * t + lax.broadcasted_iota(I32, (1, t), 1)
    row = lax.broadcasted_iota(I32, (LANE, t), 0)
    for h in range(N_HEADS):
        qz_ref[h // 2, :, (h % 2) * t:(h % 2 + 1) * t] = jnp.where(
            (row // HEAD_DIM) == (h % 2), _pair_rows(q_ref, h), 0).astype(BF16)
        iqz_ref[:, h * t:(h + 1) * t] = jnp.concatenate(
            [iq_ref[h * D_IDX:(h + 1) * D_IDX, :], jnp.zeros((LANE - D_IDX, t), BF16)], axis=0)

    iw = iw_ref[...]

    gm_ref[...] = jnp.full((t, t), INT_MIN, I32)

    def score_block(c, _, masked):
        k0 = pl.multiple_of(c * t, t)
        a = _dot(ik_ref[pl.ds(k0, t), :], iqz_ref[...])
        acc = None
        for h in range(N_HEADS):
            term = jnp.maximum(a[:, h * t:(h + 1) * t], 0.0) * iw[h:h + 1, :]
            acc = term if acc is None else acc + term
        bits = lax.bitcast_convert_type(acc, I32)
        key = jnp.where(bits < 0, bits ^ INT_MAX, bits)
        key = key - (key >> 31)
        if masked:
            key = jnp.where(keep, key, INT_MIN)
        keys_ref[c] = key
        gm_ref[...] = jnp.maximum(gm_ref[...], key)
        return 0

    lax.fori_loop(0, i, functools.partial(score_block, masked=False), 0)
    score_block(i, 0, True)

    g = gm_ref[...]
    while g.shape[0] // 2 >= top_k:
        half = g.shape[0] // 2
        g = jnp.maximum(g[:half], g[half:])
    assert g.shape[0] >= top_k

    def fold_rows(x, op):
        return op(op(x.reshape(x.shape[0] // 8, 8, t), axis=0).astype(F32), axis=0, keepdims=True)

    margin, edge = 256.0, 2.0 ** 31 - 1024.0
    lo_b = jnp.maximum(fold_rows(g, jnp.min) - margin, -edge).astype(I32)
    hi_b = jnp.minimum(fold_rows(g, jnp.max) + margin, edge).astype(I32)

    def count(pred):
        def body(c, acc):
            return acc + jnp.sum(pred(keys_ref[c], c).reshape(t // 8, 8, t), axis=0)
        acc = lax.fori_loop(0, nblk, body, jnp.zeros((8, t), F32))
        return jnp.sum(acc, axis=0, keepdims=True).astype(I32)

    n_valid = q_pos + 1
    small = n_valid <= top_k
    lo0 = jnp.full((1, t), INT_MIN + 1, I32)
    state0 = dict(lo=lo0, hi=hi_b, clo=n_valid, chi=jnp.zeros((1, t), I32),
                  thr=lo0, need=jnp.zeros((1, t), I32), tied=jnp.zeros((1, t), I32), done=small.astype(I32))

    def n_active(done):
        return jnp.sum((1 - done).astype(F32))

    def body(c):
        _, st = c
        lo, hi, clo, chi, done = st["lo"], st["hi"], st["clo"], st["chi"], st["done"]
        mid = _floor_avg(lo, hi)
        mid = jnp.where((lo == 0) & (hi > 1), 1, mid)
        first = (lo == INT_MIN + 1) & (lo_b > INT_MIN + 1)
        mid = jnp.where(first, jnp.minimum(lo_b, hi - 1), mid)
        cnt = count(lambda kc, _c: jnp.where(kc >= mid, 1.0, 0.0))
        act = done == 0
        ge = cnt >= top_k
        lo_n = jnp.where(ge, mid, lo)
        clo_n = jnp.where(ge, cnt, clo)
        hi_n = jnp.where(ge, hi, mid)
        chi_n = jnp.where(ge, chi, cnt)
        exact = cnt == top_k
        fin = act & (exact | (hi_n == lo_n + 1))
        upd = lambda new, old: jnp.where(act, new, old)
        st_n = dict(lo=upd(lo_n, lo), hi=upd(hi_n, hi), clo=upd(clo_n, clo), chi=upd(chi_n, chi),
                    thr=jnp.where(fin, jnp.where(exact, mid, lo_n), st["thr"]),
                    need=jnp.where(fin, top_k - chi_n, st["need"]),
                    tied=jnp.where(fin, ((~exact) & (clo_n > top_k)).astype(I32), st["tied"]),
                    done=jnp.where(fin, 1, done))
        return n_active(st_n["done"]), st_n

    _, st = lax.while_loop(lambda c: c[0] > 0.5, body, (n_active(state0["done"]), state0))
    thr, need, tied = st["thr"], st["need"], st["tied"] > 0

    @pl.when(jnp.sum(tied.astype(F32)) > 0.5)
    def _():
        tri = jnp.where(lax.broadcasted_iota(I32, (t, t), 1) <= key_row, 1.0, 0.0).astype(BF16)
        need_f = jnp.where(tied, need, 2 * L).astype(F32)

        def fbody(c, before):
            kc = keys_ref[c]
            e = jnp.where(kc == thr, 1.0, 0.0)
            rank = _dot(tri, e.astype(BF16)) + before
            keys_ref[c] = kc - jnp.where(rank > need_f, e, 0.0).astype(I32)
            return before + jnp.sum(e, axis=0, keepdims=True)

        lax.fori_loop(0, nblk, fbody, jnp.zeros((1, t), F32))

    def attn_block(c, carry, near):
        k0 = pl.multiple_of(c * t, t)
        kk = k_ref[pl.ds(k0, t), :]
        vt = v_ref[0, c]
        sel = keys_ref[c] >= thr
        sp = [_dot(kk[:, p * LANE:(p + 1) * LANE], qz_ref[p]) for p in range(N_HEADS // 2)]
        out = []
        for h in range(N_HEADS):
            s = sp[h // 2][:, (h % 2) * t:(h % 2 + 1) * t]
            if near:
                s = s + dtab_ref[h, i - c]
            s = jnp.where(sel, s, NEG)
            out.append(_online_update(carry[h], s, vt[h * HEAD_DIM:(h + 1) * HEAD_DIM, :]))
        return tuple(out)

    n_far = jnp.maximum(i - 1, 0)
    carry = lax.fori_loop(0, n_far, functools.partial(attn_block, near=False),
                          tuple(_softmax_init(t) for _ in range(N_HEADS)))
    carry = lax.fori_loop(n_far, nblk, functools.partial(attn_block, near=True), carry)
    ot = jnp.concatenate([acc / l for (_, l, acc) in carry], axis=0)
    o_ref[...] = ot.T.astype(o_ref.dtype)


def _dsa(qt, ka, vt, gt, dtab, B, L, t, top_k):
    nq = L // t
    return pl.pallas_call(
        functools.partial(_dsa_kernel, t=t, top_k=top_k, L=L),
        grid=(B, nq),
        in_specs=[_q_spec(L, t, QT_GROUP["s"]), _k_spec(L, KA_BLOCK["s"]), _v_spec(L, t, 2),
                  _q_spec(L, t, QT_GROUP["i"]),
                  pl.BlockSpec((L, LANE), lambda b, i: (b, IK_BLOCK)),
                  pl.BlockSpec((GT_ROWS, t), lambda b, i: (0, b * nq + i)),
                  pl.BlockSpec((N_HEADS, 2, t, t), lambda b, i: (0, 0, 0, 0))],
        out_specs=_out_spec(L, t),
        out_shape=jax.ShapeDtypeStruct((B * L, GROUP), BF16),
        scratch_shapes=[pltpu.VMEM((nq, t, t), I32), pltpu.VMEM((N_HEADS // 2, LANE, 2 * t), BF16),
                        pltpu.VMEM((LANE, N_HEADS * t), BF16), pltpu.VMEM((t, t), I32)],
        compiler_params=_cparams(("parallel", "arbitrary")),
        name="dsa",
    )(qt, ka, vt, qt, ka, gt, dtab)


def _conv_kernel(u_ref, halo_ref, w_ref, cb_ref, g_ref, b_ref, o_ref, h_ref, *, tc):
    i = pl.program_id(1)

    def glu(u):
        return u[:, :GROUP] * jax.nn.sigmoid(u[:, GROUP:])

    h_ref[pl.ds(CONV_HALO, tc), :] = glu(u_ref[...])
    h_ref[pl.ds(0, CONV_HALO), :] = jnp.where(i > 0, glu(halo_ref[...]), 0.0)
    acc = jnp.zeros((tc, GROUP), F32)
    for j in range(CONV_W):
        acc = acc + h_ref[pl.ds(CONV_HALO - (CONV_W - 1) + j, tc), :] * w_ref[j:j + 1, :]
    y = _layernorm(acc + cb_ref[...], g_ref[...], b_ref[...])
    o_ref[...] = (y * jax.nn.sigmoid(y)).astype(o_ref.dtype)


def _conv(pb, conv_w, cb, g, b, B, L, tc):
    nc = L // tc
    r = tc // CONV_HALO
    row = lambda a: a.reshape(1, GROUP)
    return pl.pallas_call(
        functools.partial(_conv_kernel, tc=tc),
        grid=(B, nc),
        in_specs=[pl.BlockSpec((tc, 2 * GROUP), lambda bb, i: (bb * nc + i, 0)),
                  pl.BlockSpec((CONV_HALO, 2 * GROUP), lambda bb, i: (jnp.maximum((bb * nc + i) * r - 1, 0), 0)),
                  pl.BlockSpec((CONV_W, GROUP), lambda bb, i: (0, 0)),
                  pl.BlockSpec((1, GROUP), lambda bb, i: (0, 0)),
                  pl.BlockSpec((1, GROUP), lambda bb, i: (0, 0)),
                  pl.BlockSpec((1, GROUP), lambda bb, i: (0, 0))],
        out_specs=pl.BlockSpec((tc, GROUP), lambda bb, i: (bb * nc + i, 0)),
        out_shape=jax.ShapeDtypeStruct((B * L, GROUP), BF16),
        scratch_shapes=[pltpu.VMEM((tc + CONV_HALO, GROUP), F32)],
        compiler_params=_cparams(("parallel", "arbitrary")),
        name="conv",
    )(pb, pb, conv_w, row(cb), row(g), row(b))


def _mix_kernel(yf_ref, yd_ref, ys_ref, yc_ref, x_ref, wo_ref, g_ref, b_ref, rw_ref, rb_ref,
                x1_ref, idx_ref, wt_ref, *, alpha):
    mix = None
    for n, y_ref in enumerate((yf_ref, yd_ref, ys_ref, yc_ref)):
        part = _dot(y_ref[...], wo_ref[n * GROUP:(n + 1) * GROUP, :])
        mix = part if mix is None else mix + part
    x1 = _layernorm(alpha * x_ref[...] + mix, g_ref[...], b_ref[...])
    x1_ref[...] = x1

    a1, a2, a3 = _split3(x1)
    w1, w2, w3 = rw_ref[0], rw_ref[1], rw_ref[2]
    lg = (_dot(a1, w1) + (_dot(a1, w2) + _dot(a2, w1)) + (_dot(a1, w3) + _dot(a2, w2) + _dot(a3, w1))) + rb_ref[...]

    tm = lg.shape[0]
    lane = lax.broadcasted_iota(I32, (tm, LANE), 1)
    lane_f = lane.astype(F32)
    idx_out = jnp.zeros((tm, LANE), I32)
    vals = []
    for r in range(TOP_K):
        mx = jnp.max(lg, axis=1, keepdims=True)
        am = jnp.min(jnp.where(lg == mx, lane_f, float(LANE)), axis=1, keepdims=True).astype(I32)
        vals.append(mx)
        idx_out = jnp.where(lane == r, am, idx_out)
        lg = jnp.where(lane == am, 2 * NEG, lg)
    es = [jnp.exp(v - vals[0]) for v in vals]
    den = es[0] + es[1] + es[2] + es[3]
    wt = jnp.zeros((tm, LANE), F32)
    for r in range(TOP_K):
        wt = jnp.where(lane == r, es[r] / den, wt)
    idx_ref[...] = idx_out
    wt_ref[...] = wt


def _mix(ys, x2d, wo, g, b, rw3, rb, alpha, tm):
    T = x2d.shape[0]
    yspec = pl.BlockSpec((tm, GROUP), lambda i: (i, 0))
    full = lambda shp: pl.BlockSpec(shp, lambda i: (0,) * len(shp))
    return pl.pallas_call(
        functools.partial(_mix_kernel, alpha=alpha),
        grid=(T // tm,),
        in_specs=[yspec, yspec, yspec, yspec,
                  pl.BlockSpec((tm, D_MODEL), lambda i: (i, 0)),
                  full((D_MODEL, D_MODEL)), full((1, D_MODEL)), full((1, D_MODEL)),
                  full((3, D_MODEL, LANE)), full((1, LANE))],
        out_specs=[pl.BlockSpec((tm, D_MODEL), lambda i: (i, 0)),
                   pl.BlockSpec((tm, LANE), lambda i: (i, 0)),
                   pl.BlockSpec((tm, LANE), lambda i: (i, 0))],
        out_shape=[jax.ShapeDtypeStruct((T, D_MODEL), F32),
                   jax.ShapeDtypeStruct((T, LANE), I32),
                   jax.ShapeDtypeStruct((T, LANE), F32)],
        compiler_params=_cparams(("parallel",)),
        name="mix",
    )(*ys, x2d, wo, g.reshape(1, D_MODEL), b.reshape(1, D_MODEL), rw3, rb)


def _start_row_gather(idx_ref, base, n, src_hbm, dst_ref, sem, inline=False):
    def body(r, _):
        pltpu.make_async_copy(src_hbm.at[pl.ds(idx_ref[base + r], 1)], dst_ref.at[pl.ds(r, 1)], sem).start()
        return 0
    if inline:
        for r in range(n):
            body(r, 0)
    else:
        lax.fori_loop(0, n, body, 0, unroll=8)


def _wait_row_gather(n, src_hbm, dst_ref, sem):
    pltpu.make_async_copy(src_hbm.at[pl.ds(0, n)], dst_ref, sem).wait()


def _expert_kernel(te_ref, tv_ref, src_ref, x_hbm, wgu_ref, bgu_ref, wd_ref, bd_ref, y_ref, xbuf, sem, *, tmm):
    i = pl.program_id(0)
    n = pl.num_programs(0)
    slot = lax.rem(i, 2)

    @pl.when(i == 0)
    def _():
        _start_row_gather(src_ref, 0, tmm, x_hbm, xbuf.at[0], sem.at[0])

    _wait_row_gather(tmm, x_hbm, xbuf.at[slot], sem.at[slot])
    prefetch = functools.partial(_start_row_gather, src_ref, (i + 1) * tmm, tmm, x_hbm, xbuf.at[1 - slot],
                                 sem.at[1 - slot])

    @pl.when(tv_ref[i] > 0)
    def _():
        prefetch(inline=True)
        xb = xbuf[slot].astype(BF16)
        gu = _dot(xb, wgu_ref[0]) + bgu_ref[0]
        gate = jnp.minimum(gu, SWIGLU_LIMIT)
        up = jnp.clip(pltpu.roll(gu, gu.shape[1] - 1, axis=1), -SWIGLU_LIMIT, SWIGLU_LIMIT)
        h = (up + 1.0) * gate * jax.nn.sigmoid(gate * SWIGLU_ALPHA)
        y_ref[...] = _dot(h.astype(BF16), wd_ref[0]) + bd_ref[0]

    @pl.when(tv_ref[i] == 0)
    def _():
        prefetch()
        y_ref[...] = jnp.zeros_like(y_ref)

    @pl.when(i == n - 1)
    def _():
        _wait_row_gather(tmm, x_hbm, xbuf.at[1 - slot], sem.at[1 - slot])


def _experts(tile_e, tile_v, src, x1, wgu, bgu, wd2, bd, R, tmm):
    return pl.pallas_call(
        functools.partial(_expert_kernel, tmm=tmm),
        grid_spec=pltpu.PrefetchScalarGridSpec(
            num_scalar_prefetch=3,
            grid=(R // tmm,),
            in_specs=[pl.BlockSpec(memory_space=pl.ANY),
                      pl.BlockSpec((1, D_MODEL, 2 * D_MODEL), lambda i, te, tv, s: (te[i], 0, 0)),
                      pl.BlockSpec((1, 1, 2 * D_MODEL), lambda i, te, tv, s: (te[i], 0, 0)),
                      pl.BlockSpec((1, 2 * D_MODEL, D_MODEL), lambda i, te, tv, s: (te[i], 0, 0)),
                      pl.BlockSpec((1, 1, D_MODEL), lambda i, te, tv, s: (te[i], 0, 0))],
            out_specs=pl.BlockSpec((tmm, D_MODEL), lambda i, te, tv, s: (i, 0)),
            scratch_shapes=[pltpu.VMEM((2, tmm, D_MODEL), F32), pltpu.SemaphoreType.DMA((2,))],
        ),
        out_shape=jax.ShapeDtypeStruct((R, D_MODEL), F32),
        compiler_params=_cparams(("arbitrary",)),
        name="moe_experts",
    )(tile_e, tile_v, src, x1, wgu, bgu, wd2, bd)


def _combine_kernel(pos_ref, ys_hbm, x1_ref, wt_ref, g_ref, b_ref, o_ref, buf, sem, *, tc, alpha):
    i = pl.program_id(0)
    n = pl.num_programs(0)
    slot = lax.rem(i, 2)
    rows = tc * TOP_K

    @pl.when(i == 0)
    def _():
        _start_row_gather(pos_ref, 0, rows, ys_hbm, buf.at[0], sem.at[0])

    @pl.when(i + 1 < n)
    def _():
        _start_row_gather(pos_ref, (i + 1) * rows, rows, ys_hbm, buf.at[1 - slot], sem.at[1 - slot], inline=True)

    _wait_row_gather(rows, ys_hbm, buf.at[slot], sem.at[slot])
    wt = wt_ref[...]
    ffn = buf[slot, pl.ds(0, tc), :] * wt[:, 0:1]
    for k in range(1, TOP_K):
        ffn = ffn + buf[slot, pl.ds(k * tc, tc), :] * wt[:, k:k + 1]
    o_ref[...] = _layernorm(alpha * x1_ref[...] + ffn, g_ref[...], b_ref[...])


def _combine(pos, ys, x1, wt, g, b, alpha, tc):
    T = x1.shape[0]
    pos_t = jnp.transpose(pos.reshape(T // tc, tc, TOP_K), (0, 2, 1)).reshape(T * TOP_K)
    return pl.pallas_call(
        functools.partial(_combine_kernel, tc=tc, alpha=alpha),
        grid_spec=pltpu.PrefetchScalarGridSpec(
            num_scalar_prefetch=1,
            grid=(T // tc,),
            in_specs=[pl.BlockSpec(memory_space=pl.ANY),
                      pl.BlockSpec((tc, D_MODEL), lambda i, p: (i, 0)),
                      pl.BlockSpec((tc, LANE), lambda i, p: (i, 0)),
                      pl.BlockSpec((1, D_MODEL), lambda i, p: (0, 0)),
                      pl.BlockSpec((1, D_MODEL), lambda i, p: (0, 0))],
            out_specs=pl.BlockSpec((tc, D_MODEL), lambda i, p: (i, 0)),
            scratch_shapes=[pltpu.VMEM((2, TOP_K * tc, D_MODEL), F32), pltpu.SemaphoreType.DMA((2,))],
        ),
        out_shape=jax.ShapeDtypeStruct((T, D_MODEL), F32),
        compiler_params=_cparams(("arbitrary",)),
        name="moe_combine",
    )(pos_t, ys, x1, wt, g.reshape(1, D_MODEL), b.reshape(1, D_MODEL))


def _moe_plan(top_idx, tmm):
    T = top_idx.shape[0]
    F = T * TOP_K
    R = F + N_EXPERTS * tmm
    flat_e = top_idx.reshape(F)
    onehot = (flat_e[:, None] == jnp.arange(N_EXPERTS, dtype=I32)[None, :]).astype(I32)
    csum = jnp.cumsum(onehot, axis=0)
    counts = csum[-1]
    rank_in_e = jnp.take_along_axis(csum, flat_e[:, None], axis=1)[:, 0] - 1
    padded = ((counts + tmm - 1) // tmm) * tmm
    gend = jnp.cumsum(padded)
    gstart = gend - padded
    ustart = jnp.cumsum(counts) - counts
    pos = gstart[flat_e] + rank_in_e
    order = jnp.argsort(flat_e, stable=True).astype(I32)
    rows = jnp.arange(R + tmm, dtype=I32)
    e_row = jnp.minimum(jnp.sum((rows[:, None] >= gend[None, :]).astype(I32), axis=1), N_EXPERTS - 1)
    off = rows - gstart[e_row]
    valid = off < counts[e_row]
    src = jnp.where(valid, order[jnp.clip(ustart[e_row] + off, 0, F - 1)] // TOP_K, 0)
    tile_rows = rows[:R:tmm]
    tile_v = (tile_rows < gend[-1]).astype(I32)
    tile_e = jnp.where(tile_v > 0, e_row[:R:tmm], e_row[jnp.maximum(gend[-1] - 1, 0)])
    return pos.astype(I32), src.astype(I32), tile_e.astype(I32), tile_v, R


def _rel_bucket(dist):
    n = jnp.maximum(dist, 0)
    max_exact = NUM_BUCKETS // 2
    nf = jnp.maximum(n, 1).astype(F32)
    large = max_exact + (jnp.log(nf / max_exact) / math.log(MAX_DISTANCE / max_exact)
                         * (NUM_BUCKETS - max_exact)).astype(I32)
    large = jnp.minimum(large, NUM_BUCKETS - 1)
    return jnp.where(n < max_exact, n, large)


def _bias_tables(table, t):
    assert t >= MAX_DISTANCE, "key blocks two or more away must lie entirely in the last bucket"
    d0 = jnp.arange(t, dtype=I32)[None, :] - jnp.arange(t, dtype=I32)[:, None]
    rel = table - table[NUM_BUCKETS - 1][None, :]
    out = []
    for d in (d0, d0 + t):
        b = jnp.transpose(rel[_rel_bucket(d)], (2, 0, 1))
        out.append(jnp.where((d >= 0)[None], b, NEG))
    return jnp.stack(out, axis=1).astype(F32)


def kernel(x, rel_bias, w_in, forget_b, diff_lambda, diff_norm_g, conv_w, conv_b, conv_ln_g, conv_ln_b,
           w_out, ln1_g, ln1_b, router_w, router_b, w_gu, b_gu, w_down, b_down, ln2_g, ln2_b):
    B, L, D = x.shape
    depth = w_in.shape[0]
    assert D == D_MODEL
    T = B * L
    t = 512 if L % 512 == 0 else 256
    tm = 256
    tmm = 256
    tcv = 512 if L % 512 == 0 else 256
    assert L % t == 0 and T % tm == 0
    top_k = min(K_SEL_MAX, L // 4)
    alpha = (2 * depth) ** 0.25

    dtab_diff = _bias_tables(rel_bias[:, :N_HEADS], t)
    dtab_dsa = _bias_tables(rel_bias[:, N_HEADS:], t)

    x2d = x.reshape(T, D)
    for l in range(depth):
        lambda_init = 0.8 - 0.6 * math.exp(-0.3 * l)
        qt, ka, vt, pb, gt = _proj(x2d, _proj_weights(w_in[l]), t)

        fb_row = jnp.zeros((1, LANE), F32).at[0, :N_HEADS].set(forget_b[l])
        caug = _foxgate(pb, fb_row, B, L)
        y_fox = _fox(qt, ka, caug, vt, B, L, t)

        lp = diff_lambda[l].astype(F32)
        lam = (jnp.exp(jnp.sum(lp[0] * lp[1])) - jnp.exp(jnp.sum(lp[2] * lp[3])) + lambda_init).reshape(1)
        g_tile = jnp.broadcast_to(diff_norm_g[l].astype(F32)[:, None], (HEAD_DIM, t))
        y_diff = _diff(lam, qt, ka, vt, dtab_diff, g_tile, B, L, t, 1.0 - lambda_init)

        y_dsa = _dsa(qt, ka, vt, gt, dtab_dsa, B, L, t, top_k)
        y_conv = _conv(pb, conv_w[l], conv_b[l], conv_ln_g[l], conv_ln_b[l], B, L, tcv)

        rw = jnp.zeros((D, LANE), F32).at[:, :N_EXPERTS].set(router_w[l])
        r1 = rw.astype(BF16)
        r2 = (rw - r1.astype(F32)).astype(BF16)
        r3 = (rw - r1.astype(F32) - r2.astype(F32)).astype(BF16)
        rb = jnp.full((1, LANE), NEG, F32).at[0, :N_EXPERTS].set(router_b[l])
        x1, top_idx, top_w = _mix((y_fox, y_diff, y_dsa, y_conv), x2d, w_out[l].astype(BF16),
                                  ln1_g[l], ln1_b[l], jnp.stack([r1, r2, r3]), rb, alpha, tm)

        pos, src, tile_e, tile_v, R = _moe_plan(top_idx[:, :TOP_K], tmm)
        wd = w_down[l].astype(BF16)
        wd2 = jnp.stack([wd, jnp.zeros_like(wd)], axis=2).reshape(N_EXPERTS, 2 * D, D)
        ys = _experts(tile_e, tile_v, src, x1, w_gu[l].astype(BF16), b_gu[l][:, None, :], wd2,
                      b_down[l][:, None, :], R, tmm)
        x2d = _combine(pos, ys, x1, top_w, ln2_g[l], ln2_b[l], alpha, 128)
    return x2d.reshape(B, L, D)
```

```python
import functools
import math

import numpy as np
import jax
import jax.numpy as jnp
from jax import lax
from jax.experimental import pallas as pl
from jax.experimental.pallas import tpu as pltpu

F32 = jnp.float32
BF16 = jnp.bfloat16
I32 = jnp.int32

D_MODEL = 1024
HEAD_DIM = 64
N_HEADS = 4
GROUP = N_HEADS * HEAD_DIM
DIFF_QK = HEAD_DIM // 2
D_IDX = 64
K_SEL_MAX = 256
CONV_W = 31
CONV_HALO = 32
NUM_BUCKETS = 32
MAX_DISTANCE = 128
N_EXPERTS = 32
TOP_K = 4
SWIGLU_LIMIT = 7.0
SWIGLU_ALPHA = 1.702
LN_EPS = 1e-5
LOG2E = math.log2(math.e)
NEG = -1e30
INT_MIN = -(2 ** 31)
INT_MAX = 2 ** 31 - 1
LANE = 128
VMEM_LIMIT = 56 * 1024 * 1024

_IN_WIDTHS = (GROUP, GROUP, GROUP, N_HEADS, GROUP, GROUP, GROUP, GROUP, GROUP, GROUP,
              N_HEADS * D_IDX, D_IDX, N_HEADS, 2 * GROUP)
_IN_OFF = tuple(int(v) for v in np.cumsum((0,) + _IN_WIDTHS))
(_FQ, _FK, _FV, _FF, _DQ, _DK, _DV, _SQ, _SK, _SV, _IQ, _IK, _IW, _CU) = range(14)

QT_GROUP = {"f": 0, "d": 1, "s": 2, "i": 3}
KA_BLOCK = {"f": 0, "d": 1, "s": 2}
NK = 3 * GROUP + LANE
IK_BLOCK = 3 * GROUP // LANE
NB = 2 * GROUP + LANE
GATE_BLOCK = 2 * GROUP // LANE
GT_ROWS = 16


def _cols(w, slots, scales=None):
    parts = []
    for n, s in enumerate(slots):
        part = w[:, _IN_OFF[s]:_IN_OFF[s + 1]]
        parts.append(part if scales is None or scales[n] == 1.0 else part * scales[n])
    return jnp.concatenate(parts, axis=1)


def _proj_weights(w):
    qs = HEAD_DIM ** -0.5 * LOG2E
    iw_scale = N_HEADS ** -0.5
    wq_t = _cols(w, (_FQ, _DQ, _SQ, _IQ), (qs, DIFF_QK ** -0.5 * LOG2E, qs, D_IDX ** -0.5)).T.astype(BF16)
    wk = jnp.pad(_cols(w, (_FK, _DK, _SK, _IK)), ((0, 0), (0, LANE - D_IDX))).astype(BF16)
    wv_t = _cols(w, (_FV, _DV, _SV)).T.astype(BF16)
    wb = jnp.pad(_cols(w, (_CU, _FF, _IW), (1.0, 1.0, iw_scale)),
                 ((0, 0), (0, LANE - 2 * N_HEADS))).astype(BF16)
    wg_t = jnp.pad((w[:, _IN_OFF[_IW]:_IN_OFF[_IW + 1]] * iw_scale).T,
                   ((0, GT_ROWS - N_HEADS), (0, 0))).astype(BF16)
    return wq_t, wk, wv_t, wb, wg_t


def _cparams(sem):
    return pltpu.CompilerParams(dimension_semantics=sem, vmem_limit_bytes=VMEM_LIMIT)


def _dot(a, b):
    return jnp.dot(a, b, preferred_element_type=F32)


def _dot_nt(a, b):
    return lax.dot_general(a, b, (((1,), (1,)), ((), ())), preferred_element_type=F32)


def _split3(a):
    a1 = a.astype(BF16)
    r1 = a - a1.astype(F32)
    a2 = r1.astype(BF16)
    a3 = (r1 - a2.astype(F32)).astype(BF16)
    return a1, a2, a3


def _layernorm(z, g, b):
    mu = jnp.mean(z, axis=-1, keepdims=True)
    zc = z - mu
    var = jnp.mean(zc * zc, axis=-1, keepdims=True)
    return zc * lax.rsqrt(var + LN_EPS) * g + b


def _proj_kernel(x_ref, wq_ref, wk_ref, wv_ref, wb_ref, wg_ref, qt_ref, ka_ref, vt_ref, pb_ref, gt_ref):
    xb = x_ref[...].astype(BF16)
    qt_ref[...] = _dot_nt(wq_ref[...], xb).astype(qt_ref.dtype)
    ka_ref[...] = _dot(xb, wk_ref[...]).astype(ka_ref.dtype)
    vt = _dot_nt(wv_ref[...], xb)
    for g in range(3):
        vt_ref[g, 0] = vt[g * GROUP:(g + 1) * GROUP, :].astype(vt_ref.dtype)
    pb_ref[...] = _dot(xb, wb_ref[...])
    gt_ref[...] = _dot_nt(wg_ref[...], xb)


def _proj(x2d, weights, t):
    T = x2d.shape[0]
    full = lambda a: pl.BlockSpec(a.shape, lambda i: (0,) * a.ndim)
    return pl.pallas_call(
        _proj_kernel,
        grid=(T // t,),
        in_specs=[pl.BlockSpec((t, D_MODEL), lambda i: (i, 0))] + [full(a) for a in weights],
        out_specs=[pl.BlockSpec((4 * GROUP, t), lambda i: (0, i)),
                   pl.BlockSpec((t, NK), lambda i: (i, 0)),
                   pl.BlockSpec((3, 1, GROUP, t), lambda i: (0, i, 0, 0)),
                   pl.BlockSpec((t, NB), lambda i: (i, 0)),
                   pl.BlockSpec((GT_ROWS, t), lambda i: (0, i))],
        out_shape=[jax.ShapeDtypeStruct((4 * GROUP, T), BF16),
                   jax.ShapeDtypeStruct((T, NK), BF16),
                   jax.ShapeDtypeStruct((3, T // t, GROUP, t), BF16),
                   jax.ShapeDtypeStruct((T, NB), F32),
                   jax.ShapeDtypeStruct((GT_ROWS, T), F32)],
        compiler_params=_cparams(("parallel",)),
        name="proj",
    )(x2d, *weights)


def _foxgate_kernel(g_ref, fb_ref, c_ref, *, chunk):
    L = g_ref.shape[0]
    row = lax.broadcasted_iota(I32, (chunk, chunk), 0)
    col = lax.broadcasted_iota(I32, (chunk, chunk), 1)
    tri = jnp.where(col <= row, 1.0, 0.0).astype(BF16)
    lane = lax.broadcasted_iota(I32, (chunk, LANE), 1)

    def body(i, carry):
        r0 = pl.multiple_of(i * chunk, chunk)
        z = g_ref[pl.ds(r0, chunk), :] + fb_ref[...]
        lf = jnp.minimum(z, 0.0) - jnp.log1p(jnp.exp(-jnp.abs(z)))
        p1, p2, p3 = _split3(lf)
        cs = _dot(tri, p1) + _dot(tri, p2) + _dot(tri, p3) + carry
        cl = jnp.where(lane < N_HEADS, cs * LOG2E, 0.0)
        c1 = cl.astype(BF16).astype(F32)
        r1 = cl - c1
        c2 = r1.astype(BF16).astype(F32)
        c3 = r1 - c2
        out = c1 + pltpu.roll(c2, N_HEADS, axis=1) + pltpu.roll(c3, 2 * N_HEADS, axis=1)
        c_ref[pl.ds(r0, chunk), :] = out.astype(c_ref.dtype)
        return cs[chunk - 1:chunk, :]

    lax.fori_loop(0, L // chunk, body, jnp.zeros((1, LANE), F32))


def _foxgate(pb, fb_row, B, L):
    return pl.pallas_call(
        functools.partial(_foxgate_kernel, chunk=256),
        grid=(B,),
        in_specs=[pl.BlockSpec((L, LANE), lambda b: (b, GATE_BLOCK)),
                  pl.BlockSpec((1, LANE), lambda b: (0, 0))],
        out_specs=pl.BlockSpec((L, LANE), lambda b: (b, 0)),
        out_shape=jax.ShapeDtypeStruct((B * L, LANE), BF16),
        compiler_params=_cparams(("parallel",)),
        name="foxgate",
    )(pb, fb_row)


def _online_update(carry, s, vt):
    m, l, acc = carry
    m_new = jnp.maximum(m, jnp.max(s, axis=0, keepdims=True))
    alpha = jnp.exp2(m - m_new)
    p = jnp.exp2(s - m_new)
    l = alpha * l + jnp.sum(p, axis=0, keepdims=True)
    acc = alpha * acc + _dot(vt, p.astype(BF16))
    return m_new, l, acc


def _softmax_init(t):
    return (jnp.full((1, t), NEG, F32), jnp.zeros((1, t), F32), jnp.zeros((HEAD_DIM, t), F32))


def _causal(t):
    return lax.broadcasted_iota(I32, (t, t), 0) <= lax.broadcasted_iota(I32, (t, t), 1)


def _pair_rows(q_ref, h):
    p = h // 2
    return q_ref[p * LANE:(p + 1) * LANE, :]


def _q_spec(L, t, group):
    nq = L // t
    return pl.BlockSpec((GROUP, t), lambda b, i: (group, b * nq + i))


def _k_spec(L, block):
    return pl.BlockSpec((L, GROUP), lambda b, i: (b, block))


def _v_spec(L, t, group):
    return pl.BlockSpec((1, L // t, GROUP, t), lambda b, i: (group, b, 0, 0))


def _out_spec(L, t):
    nq = L // t
    return pl.BlockSpec((t, GROUP), lambda b, i: (b * nq + i, 0))


def _fox_kernel(q_ref, k_ref, ca_ref, v_ref, o_ref, qa_ref, *, t):
    i = pl.program_id(1)
    row = lax.broadcasted_iota(I32, (LANE, t), 0)
    for h in range(N_HEADS):
        qz = jnp.where((row // HEAD_DIM) == (h % 2), _pair_rows(q_ref, h), 0)
        sel = jnp.where((row == h) | (row == N_HEADS + h) | (row == 2 * N_HEADS + h), -1.0, 0.0).astype(BF16)
        qa_ref[h // 2, :, (h % 2) * t:(h % 2 + 1) * t] = jnp.concatenate([qz.astype(BF16), sel], axis=0)
    keep = _causal(t)

    def step(j, carry, masked):
        k0 = pl.multiple_of(j * t, t)
        kk = k_ref[pl.ds(k0, t), :]
        ca = ca_ref[pl.ds(k0, t), :]
        vt = v_ref[0, j]
        sp = [_dot(jnp.concatenate([kk[:, p * LANE:(p + 1) * LANE], ca], axis=1), qa_ref[p])
              for p in range(N_HEADS // 2)]
        out = []
        for h in range(N_HEADS):
            s = sp[h // 2][:, (h % 2) * t:(h % 2 + 1) * t]
            if masked:
                s = jnp.where(keep, s, NEG)
            out.append(_online_update(carry[h], s, vt[h * HEAD_DIM:(h + 1) * HEAD_DIM, :]))
        return tuple(out)

    carry = lax.fori_loop(0, i, functools.partial(step, masked=False),
                          tuple(_softmax_init(t) for _ in range(N_HEADS)))
    carry = step(i, carry, True)
    ot = jnp.concatenate([acc / l for (_, l, acc) in carry], axis=0)
    o_ref[...] = ot.T.astype(o_ref.dtype)


def _fox(qt, ka, caug, vt, B, L, t):
    return pl.pallas_call(
        functools.partial(_fox_kernel, t=t),
        grid=(B, L // t),
        in_specs=[_q_spec(L, t, QT_GROUP["f"]), _k_spec(L, KA_BLOCK["f"]),
                  pl.BlockSpec((L, LANE), lambda b, i: (b, 0)), _v_spec(L, t, 0)],
        out_specs=_out_spec(L, t),
        out_shape=jax.ShapeDtypeStruct((B * L, GROUP), BF16),
        scratch_shapes=[pltpu.VMEM((N_HEADS // 2, 2 * LANE, 2 * t), BF16)],
        compiler_params=_cparams(("parallel", "arbitrary")),
        name="fox",
    )(qt, ka, caug, vt)


def _diff_kernel(lam_ref, q_ref, k_ref, v_ref, dtab_ref, g_ref, o_ref, qz_ref, *, t, out_scale):
    i = pl.program_id(1)
    lam = lam_ref[0]
    row = lax.broadcasted_iota(I32, (LANE, t), 0)
    for h in range(N_HEADS):
        for m in range(2):
            lo = (h % 2) * HEAD_DIM + m * DIFF_QK
            n = 2 * (h % 2) + m
            qz_ref[h // 2, :, n * t:(n + 1) * t] = jnp.where((row >= lo) & (row < lo + DIFF_QK),
                                                            _pair_rows(q_ref, h), 0).astype(BF16)

    pieces = []
    for p in range(N_HEADS // 2):
        def step(j, carry, near, p=p):
            k0 = pl.multiple_of(j * t, t)
            vt = v_ref[0, j]
            sp = _dot(k_ref[pl.ds(k0, t), p * LANE:(p + 1) * LANE], qz_ref[p])
            out = []
            for hh in range(2):
                h = 2 * p + hh
                for m in range(2):
                    n = 2 * hh + m
                    s = sp[:, n * t:(n + 1) * t]
                    if near:
                        s = s + dtab_ref[h, i - j]
                    out.append(_online_update(carry[2 * hh + m], s, vt[h * HEAD_DIM:(h + 1) * HEAD_DIM, :]))
            return tuple(out)

        n_far = jnp.maximum(i - 1, 0)
        carry = lax.fori_loop(0, n_far, functools.partial(step, near=False),
                              tuple(_softmax_init(t) for _ in range(4)))
        carry = lax.fori_loop(n_far, i + 1, functools.partial(step, near=True), carry)
        for hh in range(2):
            (_, l1, a1), (_, l2, a2) = carry[2 * hh], carry[2 * hh + 1]
            o = a1 / l1 - lam * (a2 / l2)
            o = o * lax.rsqrt(jnp.mean(o * o, axis=0, keepdims=True) + LN_EPS)
            pieces.append(o * g_ref[...] * out_scale)
    o_ref[...] = jnp.concatenate(pieces, axis=0).T.astype(o_ref.dtype)


def _diff(lam, qt, ka, vt, dtab, g_tile, B, L, t, out_scale):
    return pl.pallas_call(
        functools.partial(_diff_kernel, t=t, out_scale=out_scale),
        grid=(B, L // t),
        in_specs=[pl.BlockSpec(memory_space=pltpu.SMEM),
                  _q_spec(L, t, QT_GROUP["d"]), _k_spec(L, KA_BLOCK["d"]), _v_spec(L, t, 1),
                  pl.BlockSpec((N_HEADS, 2, t, t), lambda b, i: (0, 0, 0, 0)),
                  pl.BlockSpec((HEAD_DIM, t), lambda b, i: (0, 0))],
        out_specs=_out_spec(L, t),
        out_shape=jax.ShapeDtypeStruct((B * L, GROUP), BF16),
        scratch_shapes=[pltpu.VMEM((N_HEADS // 2, LANE, 4 * t), BF16)],
        compiler_params=_cparams(("parallel", "arbitrary")),
        name="diff",
    )(lam, qt, ka, vt, dtab, g_tile)


def _floor_avg(a, b):
    return (a >> 1) + (b >> 1) + (a & b & 1)


def _dsa_kernel(q_ref, k_ref, v_ref, iq_ref, ik_ref, iw_ref, dtab_ref, o_ref, keys_ref, qz_ref, iqz_ref, gm_ref,
                *, t, top_k, L):
    i = pl.program_id(1)
    nblk = i + 1
    keep = _causal(t)
    key_row = lax.broadcasted_iota(I32, (t, t), 0)
    q_pos = i * t + lax.broadcasted_iota(I32, (1, t), 1)
    row = lax.broadcasted_iota(I32, (LANE, t), 0)
    for h in range(N_HEADS):
        qz_ref[h // 2, :, (h % 2) * t:(h % 2 + 1) * t] = jnp.where(
            (row // HEAD_DIM) == (h % 2), _pair_rows(q_ref, h), 0).astype(BF16)
        iqz_ref[:, h * t:(h + 1) * t] = jnp.concatenate(
            [iq_ref[h * D_IDX:(h + 1) * D_IDX, :], jnp.zeros((LANE - D_IDX, t), BF16)], axis=0)

    iw = iw_ref[...]

    gm_ref[...] = jnp.full((t, t), INT_MIN, I32)

    def score_block(c, _, masked):
        k0 = pl.multiple_of(c * t, t)
        a = _dot(ik_ref[pl.ds(k0, t), :], iqz_ref[...])
        acc = None
        for h in range(N_HEADS):
            term = jnp.maximum(a[:, h * t:(h + 1) * t], 0.0) * iw[h:h + 1, :]
            acc = term if acc is None else acc + term
        bits = lax.bitcast_convert_type(acc, I32)
        key = jnp.where(bits < 0, bits ^ INT_MAX, bits)
        key = key - (key >> 31)
        if masked:
            key = jnp.where(keep, key, INT_MIN)
        keys_ref[c] = key
        gm_ref[...] = jnp.maximum(gm_ref[...], key)
        return 0

    lax.fori_loop(0, i, functools.partial(score_block, masked=False), 0)
    score_block(i, 0, True)

    g = gm_ref[...]
    while g.shape[0] // 2 >= top_k:
        half = g.shape[0] // 2
        g = jnp.maximum(g[:half], g[half:])
    assert g.shape[0] >= top_k

    def fold_rows(x, op):
        return op(op(x.reshape(x.shape[0] // 8, 8, t), axis=0).astype(F32), axis=0, keepdims=True)

    margin, edge = 256.0, 2.0 ** 31 - 1024.0
    lo_b = jnp.maximum(fold_rows(g, jnp.min) - margin, -edge).astype(I32)
    hi_b = jnp.minimum(fold_rows(g, jnp.max) + margin, edge).astype(I32)

    def count(pred):
        def body(c, acc):
            return acc + jnp.sum(pred(keys_ref[c], c).reshape(t // 8, 8, t), axis=0)
        acc = lax.fori_loop(0, nblk, body, jnp.zeros((8, t), F32))
        return jnp.sum(acc, axis=0, keepdims=True).astype(I32)

    n_valid = q_pos + 1
    small = n_valid <= top_k
    lo0 = jnp.full((1, t), INT_MIN + 1, I32)
    state0 = dict(lo=lo0, hi=hi_b, clo=n_valid, chi=jnp.zeros((1, t), I32),
                  thr=lo0, need=jnp.zeros((1, t), I32), tied=jnp.zeros((1, t), I32), done=small.astype(I32))

    def n_active(done):
        return jnp.sum((1 - done).astype(F32))

    def body(c):
        _, st = c
        lo, hi, clo, chi, done = st["lo"], st["hi"], st["clo"], st["chi"], st["done"]
        mid = _floor_avg(lo, hi)
        mid = jnp.where((lo == 0) & (hi > 1), 1, mid)
        first = (lo == INT_MIN + 1) & (lo_b > INT_MIN + 1)
        mid = jnp.where(first, jnp.minimum(lo_b, hi - 1), mid)
        cnt = count(lambda kc, _c: jnp.where(kc >= mid, 1.0, 0.0))
        act = done == 0
        ge = cnt >= top_k
        lo_n = jnp.where(ge, mid, lo)
        clo_n = jnp.where(ge, cnt, clo)
        hi_n = jnp.where(ge, hi, mid)
        chi_n = jnp.where(ge, chi, cnt)
        exact = cnt == top_k
        fin = act & (exact | (hi_n == lo_n + 1))
        upd = lambda new, old: jnp.where(act, new, old)
        st_n = dict(lo=upd(lo_n, lo), hi=upd(hi_n, hi), clo=upd(clo_n, clo), chi=upd(chi_n, chi),
                    thr=jnp.where(fin, jnp.where(exact, mid, lo_n), st["thr"]),
                    need=jnp.where(fin, top_k - chi_n, st["need"]),
                    tied=jnp.where(fin, ((~exact) & (clo_n > top_k)).astype(I32), st["tied"]),
                    done=jnp.where(fin, 1, done))
        return n_active(st_n["done"]), st_n

    _, st = lax.while_loop(lambda c: c[0] > 0.5, body, (n_active(state0["done"]), state0))
    thr, need, tied = st["thr"], st["need"], st["tied"] > 0

    @pl.when(jnp.sum(tied.astype(F32)) > 0.5)
    def _():
        tri = jnp.where(lax.broadcasted_iota(I32, (t, t), 1) <= key_row, 1.0, 0.0).astype(BF16)
        need_f = jnp.where(tied, need, 2 * L).astype(F32)

        def fbody(c, before):
            kc = keys_ref[c]
            e = jnp.where(kc == thr, 1.0, 0.0)
            rank = _dot(tri, e.astype(BF16)) + before
            keys_ref[c] = kc - jnp.where(rank > need_f, e, 0.0).astype(I32)
            return before + jnp.sum(e, axis=0, keepdims=True)

        lax.fori_loop(0, nblk, fbody, jnp.zeros((1, t), F32))

    def attn_block(c, carry, near):
        k0 = pl.multiple_of(c * t, t)
        kk = k_ref[pl.ds(k0, t), :]
        vt = v_ref[0, c]
        sel = keys_ref[c] >= thr
        sp = [_dot(kk[:, p * LANE:(p + 1) * LANE], qz_ref[p]) for p in range(N_HEADS // 2)]
        out = []
        for h in range(N_HEADS):
            s = sp[h // 2][:, (h % 2) * t:(h % 2 + 1) * t]
            if near:
                s = s + dtab_ref[h, i - c]
            s = jnp.where(sel, s, NEG)
            out.append(_online_update(carry[h], s, vt[h * HEAD_DIM:(h + 1) * HEAD_DIM, :]))
        return tuple(out)

    n_far = jnp.maximum(i - 1, 0)
    carry = lax.fori_loop(0, n_far, functools.partial(attn_block, near=False),
                          tuple(_softmax_init(t) for _ in range(N_HEADS)))
    carry = lax.fori_loop(n_far, nblk, functools.partial(attn_block, near=True), carry)
    ot = jnp.concatenate([acc / l for (_, l, acc) in carry], axis=0)
    o_ref[...] = ot.T.astype(o_ref.dtype)


def _dsa(qt, ka, vt, gt, dtab, B, L, t, top_k):
    nq = L // t
    return pl.pallas_call(
        functools.partial(_dsa_kernel, t=t, top_k=top_k, L=L),
        grid=(B, nq),
        in_specs=[_q_spec(L, t, QT_GROUP["s"]), _k_spec(L, KA_BLOCK["s"]), _v_spec(L, t, 2),
                  _q_spec(L, t, QT_GROUP["i"]),
                  pl.BlockSpec((L, LANE), lambda b, i: (b, IK_BLOCK)),
                  pl.BlockSpec((GT_ROWS, t), lambda b, i: (0, b * nq + i)),
                  pl.BlockSpec((N_HEADS, 2, t, t), lambda b, i: (0, 0, 0, 0))],
        out_specs=_out_spec(L, t),
        out_shape=jax.ShapeDtypeStruct((B * L, GROUP), BF16),
        scratch_shapes=[pltpu.VMEM((nq, t, t), I32), pltpu.VMEM((N_HEADS // 2, LANE, 2 * t), BF16),
                        pltpu.VMEM((LANE, N_HEADS * t), BF16), pltpu.VMEM((t, t), I32)],
        compiler_params=_cparams(("parallel", "arbitrary")),
        name="dsa",
    )(qt, ka, vt, qt, ka, gt, dtab)


def _conv_kernel(u_ref, halo_ref, w_ref, cb_ref, g_ref, b_ref, o_ref, h_ref, *, tc):
    i = pl.program_id(1)

    def glu(u):
        return u[:, :GROUP] * jax.nn.sigmoid(u[:, GROUP:])

    h_ref[pl.ds(CONV_HALO, tc), :] = glu(u_ref[...])
    h_ref[pl.ds(0, CONV_HALO), :] = jnp.where(i > 0, glu(halo_ref[...]), 0.0)
    acc = jnp.zeros((tc, GROUP), F32)
    for j in range(CONV_W):
        acc = acc + h_ref[pl.ds(CONV_HALO - (CONV_W - 1) + j, tc), :] * w_ref[j:j + 1, :]
    y = _layernorm(acc + cb_ref[...], g_ref[...], b_ref[...])
    o_ref[...] = (y * jax.nn.sigmoid(y)).astype(o_ref.dtype)


def _conv(pb, conv_w, cb, g, b, B, L, tc):
    nc = L // tc
    r = tc // CONV_HALO
    row = lambda a: a.reshape(1, GROUP)
    return pl.pallas_call(
        functools.partial(_conv_kernel, tc=tc),
        grid=(B, nc),
        in_specs=[pl.BlockSpec((tc, 2 * GROUP), lambda bb, i: (bb * nc + i, 0)),
                  pl.BlockSpec((CONV_HALO, 2 * GROUP), lambda bb, i: (jnp.maximum((bb * nc + i) * r - 1, 0), 0)),
                  pl.BlockSpec((CONV_W, GROUP), lambda bb, i: (0, 0)),
                  pl.BlockSpec((1, GROUP), lambda bb, i: (0, 0)),
                  pl.BlockSpec((1, GROUP), lambda bb, i: (0, 0)),
                  pl.BlockSpec((1, GROUP), lambda bb, i: (0, 0))],
        out_specs=pl.BlockSpec((tc, GROUP), lambda bb, i: (bb * nc + i, 0)),
        out_shape=jax.ShapeDtypeStruct((B * L, GROUP), BF16),
        scratch_shapes=[pltpu.VMEM((tc + CONV_HALO, GROUP), F32)],
        compiler_params=_cparams(("parallel", "arbitrary")),
        name="conv",
    )(pb, pb, conv_w, row(cb), row(g), row(b))


def _mix_kernel(yf_ref, yd_ref, ys_ref, yc_ref, x_ref, wo_ref, g_ref, b_ref, rw_ref, rb_ref,
                x1_ref, idx_ref, wt_ref, *, alpha):
    mix = None
    for n, y_ref in enumerate((yf_ref, yd_ref, ys_ref, yc_ref)):
        part = _dot(y_ref[...], wo_ref[n * GROUP:(n + 1) * GROUP, :])
        mix = part if mix is None else mix + part
    x1 = _layernorm(alpha * x_ref[...] + mix, g_ref[...], b_ref[...])
    x1_ref[...] = x1

    a1, a2, a3 = _split3(x1)
    w1, w2, w3 = rw_ref[0], rw_ref[1], rw_ref[2]
    lg = (_dot(a1, w1) + (_dot(a1, w2) + _dot(a2, w1)) + (_dot(a1, w3) + _dot(a2, w2) + _dot(a3, w1))) + rb_ref[...]

    tm = lg.shape[0]
    lane = lax.broadcasted_iota(I32, (tm, LANE), 1)
    lane_f = lane.astype(F32)
    idx_out = jnp.zeros((tm, LANE), I32)
    vals = []
    for r in range(TOP_K):
        mx = jnp.max(lg, axis=1, keepdims=True)
        am = jnp.min(jnp.where(lg == mx, lane_f, float(LANE)), axis=1, keepdims=True).astype(I32)
        vals.append(mx)
        idx_out = jnp.where(lane == r, am, idx_out)
        lg = jnp.where(lane == am, 2 * NEG, lg)
    es = [jnp.exp(v - vals[0]) for v in vals]
    den = es[0] + es[1] + es[2] + es[3]
    wt = jnp.zeros((tm, LANE), F32)
    for r in range(TOP_K):
        wt = jnp.where(lane == r, es[r] / den, wt)
    idx_ref[...] = idx_out
    wt_ref[...] = wt


def _mix(ys, x2d, wo, g, b, rw3, rb, alpha, tm):
    T = x2d.shape[0]
    yspec = pl.BlockSpec((tm, GROUP), lambda i: (i, 0))
    full = lambda shp: pl.BlockSpec(shp, lambda i: (0,) * len(shp))
    return pl.pallas_call(
        functools.partial(_mix_kernel, alpha=alpha),
        grid=(T // tm,),
        in_specs=[yspec, yspec, yspec, yspec,
                  pl.BlockSpec((tm, D_MODEL), lambda i: (i, 0)),
                  full((D_MODEL, D_MODEL)), full((1, D_MODEL)), full((1, D_MODEL)),
                  full((3, D_MODEL, LANE)), full((1, LANE))],
        out_specs=[pl.BlockSpec((tm, D_MODEL), lambda i: (i, 0)),
                   pl.BlockSpec((tm, LANE), lambda i: (i, 0)),
                   pl.BlockSpec((tm, LANE), lambda i: (i, 0))],
        out_shape=[jax.ShapeDtypeStruct((T, D_MODEL), F32),
                   jax.ShapeDtypeStruct((T, LANE), I32),
                   jax.ShapeDtypeStruct((T, LANE), F32)],
        compiler_params=_cparams(("parallel",)),
        name="mix",
    )(*ys, x2d, wo, g.reshape(1, D_MODEL), b.reshape(1, D_MODEL), rw3, rb)


def _start_row_gather(idx_ref, base, n, src_hbm, dst_ref, sem, inline=False):
    def body(r, _):
        pltpu.make_async_copy(src_hbm.at[pl.ds(idx_ref[base + r], 1)], dst_ref.at[pl.ds(r, 1)], sem).start()
        return 0
    if inline:
        for r in range(n):
            body(r, 0)
    else:
        lax.fori_loop(0, n, body, 0, unroll=8)


def _wait_row_gather(n, src_hbm, dst_ref, sem):
    pltpu.make_async_copy(src_hbm.at[pl.ds(0, n)], dst_ref, sem).wait()


def _expert_kernel(te_ref, tv_ref, src_ref, x_hbm, wt_ref, bg_ref, bu_ref, wd_ref, bd_ref, y_ref,
                   xbuf, sem, wg_s, wu_s, wd_s, *, tmm):
    i = pl.program_id(0)
    n = pl.num_programs(0)
    slot = lax.rem(i, 2)

    @pl.when(i == 0)
    def _():
        _start_row_gather(src_ref, 0, tmm, x_hbm, xbuf.at[0], sem.at[0])

    _wait_row_gather(tmm, x_hbm, xbuf.at[slot], sem.at[slot])
    prefetch = functools.partial(_start_row_gather, src_ref, (i + 1) * tmm, tmm, x_hbm, xbuf.at[1 - slot],
                                 sem.at[1 - slot])

    @pl.when((tv_ref[i] > 0) & ((i == 0) | (te_ref[i] != te_ref[jnp.maximum(i - 1, 0)])))
    def _():
        for c in range(D_MODEL // LANE):
            cols = slice(c * LANE, (c + 1) * LANE)
            wg_s[:, cols] = wt_ref[0, c, pl.ds(0, D_MODEL, stride=2), :].astype(BF16)
            wu_s[:, cols] = wt_ref[0, c, pl.ds(1, D_MODEL, stride=2), :].astype(BF16)
        wd_s[...] = wd_ref[0].astype(BF16)

    @pl.when(tv_ref[i] > 0)
    def _():
        prefetch(inline=True)
        xb = xbuf[slot].astype(BF16)
        gate = jnp.minimum(_dot_nt(xb, wg_s[...]) + bg_ref[0], SWIGLU_LIMIT)
        up = jnp.clip(_dot_nt(xb, wu_s[...]) + bu_ref[0], -SWIGLU_LIMIT, SWIGLU_LIMIT)
        h = (up + 1.0) * gate * jax.nn.sigmoid(gate * SWIGLU_ALPHA)
        y_ref[...] = _dot(h.astype(BF16), wd_s[...]) + bd_ref[0]

    @pl.when(tv_ref[i] == 0)
    def _():
        prefetch()
        y_ref[...] = jnp.zeros_like(y_ref)

    @pl.when(i == n - 1)
    def _():
        _wait_row_gather(tmm, x_hbm, xbuf.at[1 - slot], sem.at[1 - slot])


def _experts(tile_e, tile_v, src, x1, wgu_t, bg, bu, wd, bd, R, tmm):
    by_expert = lambda shp: pl.BlockSpec((1,) + shp, lambda i, te, tv, s: (te[i],) + (0,) * len(shp))
    wscratch = pltpu.VMEM((D_MODEL, D_MODEL), BF16)
    return pl.pallas_call(
        functools.partial(_expert_kernel, tmm=tmm),
        grid_spec=pltpu.PrefetchScalarGridSpec(
            num_scalar_prefetch=3,
            grid=(R // tmm,),
            in_specs=[pl.BlockSpec(memory_space=pl.ANY),
                      by_expert((D_MODEL // LANE, 2 * D_MODEL, LANE)), by_expert((1, D_MODEL)),
                      by_expert((1, D_MODEL)),
                      by_expert((D_MODEL, D_MODEL)), by_expert((1, D_MODEL))],
            out_specs=pl.BlockSpec((tmm, D_MODEL), lambda i, te, tv, s: (i, 0)),
            scratch_shapes=[pltpu.VMEM((2, tmm, D_MODEL), F32), pltpu.SemaphoreType.DMA((2,)),
                            wscratch, wscratch, wscratch],
        ),
        out_shape=jax.ShapeDtypeStruct((R, D_MODEL), F32),
        compiler_params=_cparams(("arbitrary",)),
        name="moe_experts",
    )(tile_e, tile_v, src, x1, wgu_t, bg, bu, wd, bd)


def _combine_kernel(pos_ref, ys_hbm, x1_ref, wt_ref, g_ref, b_ref, o_ref, buf, sem, *, tc, alpha):
    i = pl.program_id(0)
    n = pl.num_programs(0)
    slot = lax.rem(i, 2)
    rows = tc * TOP_K

    @pl.when(i == 0)
    def _():
        _start_row_gather(pos_ref, 0, rows, ys_hbm, buf.at[0], sem.at[0])

    @pl.when(i + 1 < n)
    def _():
        _start_row_gather(pos_ref, (i + 1) * rows, rows, ys_hbm, buf.at[1 - slot], sem.at[1 - slot], inline=True)

    _wait_row_gather(rows, ys_hbm, buf.at[slot], sem.at[slot])
    wt = wt_ref[...]
    ffn = buf[slot, pl.ds(0, tc), :] * wt[:, 0:1]
    for k in range(1, TOP_K):
        ffn = ffn + buf[slot, pl.ds(k * tc, tc), :] * wt[:, k:k + 1]
    o_ref[...] = _layernorm(alpha * x1_ref[...] + ffn, g_ref[...], b_ref[...])


def _combine(pos, ys, x1, wt, g, b, alpha, tc):
    T = x1.shape[0]
    pos_t = jnp.transpose(pos.reshape(T // tc, tc, TOP_K), (0, 2, 1)).reshape(T * TOP_K)
    return pl.pallas_call(
        functools.partial(_combine_kernel, tc=tc, alpha=alpha),
        grid_spec=pltpu.PrefetchScalarGridSpec(
            num_scalar_prefetch=1,
            grid=(T // tc,),
            in_specs=[pl.BlockSpec(memory_space=pl.ANY),
                      pl.BlockSpec((tc, D_MODEL), lambda i, p: (i, 0)),
                      pl.BlockSpec((tc, LANE), lambda i, p: (i, 0)),
                      pl.BlockSpec((1, D_MODEL), lambda i, p: (0, 0)),
                      pl.BlockSpec((1, D_MODEL), lambda i, p: (0, 0))],
            out_specs=pl.BlockSpec((tc, D_MODEL), lambda i, p: (i, 0)),
            scratch_shapes=[pltpu.VMEM((2, TOP_K * tc, D_MODEL), F32), pltpu.SemaphoreType.DMA((2,))],
        ),
        out_shape=jax.ShapeDtypeStruct((T, D_MODEL), F32),
        compiler_params=_cparams(("arbitrary",)),
        name="moe_combine",
    )(pos_t, ys, x1, wt, g.reshape(1, D_MODEL), b.reshape(1, D_MODEL))


def _moe_plan(top_idx, tmm):
    T = top_idx.shape[0]
    F = T * TOP_K
    R = F + N_EXPERTS * tmm
    flat_e = top_idx.reshape(F)
    onehot = (flat_e[:, None] == jnp.arange(N_EXPERTS, dtype=I32)[None, :]).astype(I32)
    csum = jnp.cumsum(onehot, axis=0)
    counts = csum[-1]
    padded = ((counts + tmm - 1) // tmm) * tmm
    gend = jnp.cumsum(padded)
    gstart = gend - padded
    ustart = jnp.cumsum(counts) - counts
    pos = jnp.sum(onehot * (csum - 1 + gstart[None, :]), axis=1)
    order = jnp.argsort(flat_e, stable=True).astype(I32)
    n_tiles = R // tmm + 1
    t_start = jnp.arange(n_tiles, dtype=I32) * tmm
    t_e = jnp.minimum(jnp.sum((t_start[:, None] >= gend[None, :]).astype(I32), axis=1), N_EXPERTS - 1)
    off = (t_start - gstart[t_e])[:, None] + jnp.arange(tmm, dtype=I32)[None, :]
    valid = off < counts[t_e][:, None]
    rank = jnp.clip(ustart[t_e][:, None] + off, 0, F - 1).reshape(n_tiles * tmm)
    src = jnp.where(valid.reshape(n_tiles * tmm), order[rank] // TOP_K, 0)
    tile_v = (t_start[:-1] < gend[-1]).astype(I32)
    last_e = jnp.sum((gend[-1] - 1 >= gend).astype(I32))
    tile_e = jnp.where(tile_v > 0, t_e[:-1], last_e)
    return pos.astype(I32), src.astype(I32), tile_e.astype(I32), tile_v, R


def _rel_bucket(dist):
    n = jnp.maximum(dist, 0)
    max_exact = NUM_BUCKETS // 2
    nf = jnp.maximum(n, 1).astype(F32)
    large = max_exact + (jnp.log(nf / max_exact) / math.log(MAX_DISTANCE / max_exact)
                         * (NUM_BUCKETS - max_exact)).astype(I32)
    large = jnp.minimum(large, NUM_BUCKETS - 1)
    return jnp.where(n < max_exact, n, large)


def _bias_tables(table, t):
    assert t >= MAX_DISTANCE, "key blocks two or more away must lie entirely in the last bucket"
    d0 = jnp.arange(t, dtype=I32)[None, :] - jnp.arange(t, dtype=I32)[:, None]
    rel = (table - table[NUM_BUCKETS - 1][None, :]) * LOG2E
    out = []
    for d in (d0, d0 + t):
        b = jnp.transpose(rel[_rel_bucket(d)], (2, 0, 1))
        out.append(jnp.where((d >= 0)[None], b, NEG))
    return jnp.stack(out, axis=1).astype(F32)


def kernel(x, rel_bias, w_in, forget_b, diff_lambda, diff_norm_g, conv_w, conv_b, conv_ln_g, conv_ln_b,
           w_out, ln1_g, ln1_b, router_w, router_b, w_gu, b_gu, w_down, b_down, ln2_g, ln2_b):
    B, L, D = x.shape
    depth = w_in.shape[0]
    assert D == D_MODEL
    T = B * L
    t = 512 if L % 512 == 0 else 256
    tm = 256
    tmm = 256
    tcv = 512 if L % 512 == 0 else 256
    assert L % t == 0 and T % tm == 0
    top_k = min(K_SEL_MAX, L // 4)
    alpha = (2 * depth) ** 0.25

    dtab_diff = _bias_tables(rel_bias[:, :N_HEADS], t)
    dtab_dsa = _bias_tables(rel_bias[:, N_HEADS:], t)

    x2d = x.reshape(T, D)
    for l in range(depth):
        lambda_init = 0.8 - 0.6 * math.exp(-0.3 * l)
        qt, ka, vt, pb, gt = _proj(x2d, _proj_weights(w_in[l]), t)

        fb_row = jnp.zeros((1, LANE), F32).at[0, :N_HEADS].set(forget_b[l])
        caug = _foxgate(pb, fb_row, B, L)
        y_fox = _fox(qt, ka, caug, vt, B, L, t)

        lp = diff_lambda[l].astype(F32)
        lam = (jnp.exp(jnp.sum(lp[0] * lp[1])) - jnp.exp(jnp.sum(lp[2] * lp[3])) + lambda_init).reshape(1)
        g_tile = jnp.broadcast_to(diff_norm_g[l].astype(F32)[:, None], (HEAD_DIM, t))
        y_diff = _diff(lam, qt, ka, vt, dtab_diff, g_tile, B, L, t, 1.0 - lambda_init)

        y_dsa = _dsa(qt, ka, vt, gt, dtab_dsa, B, L, t, top_k)
        y_conv = _conv(pb, conv_w[l], conv_b[l], conv_ln_g[l], conv_ln_b[l], B, L, tcv)

        rw = jnp.zeros((D, LANE), F32).at[:, :N_EXPERTS].set(router_w[l])
        r1 = rw.astype(BF16)
        r2 = (rw - r1.astype(F32)).astype(BF16)
        r3 = (rw - r1.astype(F32) - r2.astype(F32)).astype(BF16)
        rb = jnp.full((1, LANE), NEG, F32).at[0, :N_EXPERTS].set(router_b[l])
        x1, top_idx, top_w = _mix((y_fox, y_diff, y_dsa, y_conv), x2d, w_out[l].astype(BF16),
                                  ln1_g[l], ln1_b[l], jnp.stack([r1, r2, r3]), rb, alpha, tm)

        pos, src, tile_e, tile_v, R = _moe_plan(top_idx[:, :TOP_K], tmm)
        wgu_t = jnp.transpose(w_gu[l].reshape(N_EXPERTS, D // LANE, LANE, 2 * D), (0, 1, 3, 2))
        ys = _experts(tile_e, tile_v, src, x1, wgu_t, b_gu[l][:, None, 0::2],
                      b_gu[l][:, None, 1::2], w_down[l], b_down[l][:, None, :], R, tmm)
        x2d = _combine(pos, ys, x1, top_w, ln2_g[l], ln2_b[l], alpha, 128)
    return x2d.reshape(B, L, D)
```

```python
import functools
import math

import numpy as np
import jax
import jax.numpy as jnp
from jax import lax
from jax.experimental import pallas as pl
from jax.experimental.pallas import tpu as pltpu

F32 = jnp.float32
BF16 = jnp.bfloat16
I32 = jnp.int32

D_MODEL = 1024
HEAD_DIM = 64
N_HEADS = 4
GROUP = N_HEADS * HEAD_DIM
DIFF_QK = HEAD_DIM // 2
D_IDX = 64
K_SEL_MAX = 256
CONV_W = 31
CONV_HALO = 32
NUM_BUCKETS = 32
MAX_DISTANCE = 128
N_EXPERTS = 32
TOP_K = 4
SWIGLU_LIMIT = 7.0
SWIGLU_ALPHA = 1.702
LN_EPS = 1e-5
LOG2E = math.log2(math.e)
NEG = -1e30
INT_MIN = -(2 ** 31)
INT_MAX = 2 ** 31 - 1
LANE = 128
VMEM_LIMIT = 56 * 1024 * 1024

_IN_WIDTHS = (GROUP, GROUP, GROUP, N_HEADS, GROUP, GROUP, GROUP, GROUP, GROUP, GROUP,
              N_HEADS * D_IDX, D_IDX, N_HEADS, 2 * GROUP)
_IN_OFF = tuple(int(v) for v in np.cumsum((0,) + _IN_WIDTHS))
(_FQ, _FK, _FV, _FF, _DQ, _DK, _DV, _SQ, _SK, _SV, _IQ, _IK, _IW, _CU) = range(14)

QT_GROUP = {"f": 0, "d": 1, "s": 2, "i": 3}
KA_BLOCK = {"f": 0, "d": 1, "s": 2}
NK = 3 * GROUP + LANE
IK_BLOCK = 3 * GROUP // LANE
NB = 2 * GROUP + LANE
GATE_BLOCK = 2 * GROUP // LANE
GT_ROWS = 16


def _cols(w, slots, scales=None):
    parts = []
    for n, s in enumerate(slots):
        part = w[:, _IN_OFF[s]:_IN_OFF[s + 1]]
        parts.append(part if scales is None or scales[n] == 1.0 else part * scales[n])
    return jnp.concatenate(parts, axis=1)


def _proj_weights(w):
    qs = HEAD_DIM ** -0.5 * LOG2E
    iw_scale = N_HEADS ** -0.5
    wq_t = _cols(w, (_FQ, _DQ, _SQ, _IQ), (qs, DIFF_QK ** -0.5 * LOG2E, qs, D_IDX ** -0.5)).T.astype(BF16)
    wk = jnp.pad(_cols(w, (_FK, _DK, _SK, _IK)), ((0, 0), (0, LANE - D_IDX))).astype(BF16)
    wv_t = _cols(w, (_FV, _DV, _SV)).T.astype(BF16)
    wb = jnp.pad(_cols(w, (_CU, _FF, _IW), (1.0, 1.0, iw_scale)),
                 ((0, 0), (0, LANE - 2 * N_HEADS))).astype(BF16)
    wg_t = jnp.pad((w[:, _IN_OFF[_IW]:_IN_OFF[_IW + 1]] * iw_scale).T,
                   ((0, GT_ROWS - N_HEADS), (0, 0))).astype(BF16)
    return wq_t, wk, wv_t, wb, wg_t


def _cparams(sem):
    return pltpu.CompilerParams(dimension_semantics=sem, vmem_limit_bytes=VMEM_LIMIT)


def _dot(a, b):
    return jnp.dot(a, b, preferred_element_type=F32)


def _dot_nt(a, b):
    return lax.dot_general(a, b, (((1,), (1,)), ((), ())), preferred_element_type=F32)


def _split3(a):
    a1 = a.astype(BF16)
    r1 = a - a1.astype(F32)
    a2 = r1.astype(BF16)
    a3 = (r1 - a2.astype(F32)).astype(BF16)
    return a1, a2, a3


def _layernorm(z, g, b):
    mu = jnp.mean(z, axis=-1, keepdims=True)
    zc = z - mu
    var = jnp.mean(zc * zc, axis=-1, keepdims=True)
    return zc * lax.rsqrt(var + LN_EPS) * g + b


def _proj_kernel(x_ref, wq_ref, wk_ref, wv_ref, wb_ref, wg_ref, qt_ref, ka_ref, vt_ref, pb_ref, gt_ref):
    xb = x_ref[...].astype(BF16)
    qt_ref[...] = _dot_nt(wq_ref[...], xb).astype(qt_ref.dtype)
    ka_ref[...] = _dot(xb, wk_ref[...]).astype(ka_ref.dtype)
    vt = _dot_nt(wv_ref[...], xb)
    for g in range(3):
        vt_ref[g, 0] = vt[g * GROUP:(g + 1) * GROUP, :].astype(vt_ref.dtype)
    pb_ref[...] = _dot(xb, wb_ref[...])
    gt_ref[...] = _dot_nt(wg_ref[...], xb)


def _proj(x2d, weights, t):
    T = x2d.shape[0]
    full = lambda a: pl.BlockSpec(a.shape, lambda i: (0,) * a.ndim)
    return pl.pallas_call(
        _proj_kernel,
        grid=(T // t,),
        in_specs=[pl.BlockSpec((t, D_MODEL), lambda i: (i, 0))] + [full(a) for a in weights],
        out_specs=[pl.BlockSpec((4 * GROUP, t), lambda i: (0, i)),
                   pl.BlockSpec((t, NK), lambda i: (i, 0)),
                   pl.BlockSpec((3, 1, GROUP, t), lambda i: (0, i, 0, 0)),
                   pl.BlockSpec((t, NB), lambda i: (i, 0)),
                   pl.BlockSpec((GT_ROWS, t), lambda i: (0, i))],
        out_shape=[jax.ShapeDtypeStruct((4 * GROUP, T), BF16),
                   jax.ShapeDtypeStruct((T, NK), BF16),
                   jax.ShapeDtypeStruct((3, T // t, GROUP, t), BF16),
                   jax.ShapeDtypeStruct((T, NB), F32),
                   jax.ShapeDtypeStruct((GT_ROWS, T), F32)],
        compiler_params=_cparams(("parallel",)),
        name="proj",
    )(x2d, *weights)


def _foxgate_kernel(g_ref, fb_ref, c_ref, *, chunk):
    L = g_ref.shape[0]
    row = lax.broadcasted_iota(I32, (chunk, chunk), 0)
    col = lax.broadcasted_iota(I32, (chunk, chunk), 1)
    tri = jnp.where(col <= row, 1.0, 0.0).astype(BF16)
    lane = lax.broadcasted_iota(I32, (chunk, LANE), 1)

    def body(i, carry):
        r0 = pl.multiple_of(i * chunk, chunk)
        z = g_ref[pl.ds(r0, chunk), :] + fb_ref[...]
        lf = jnp.minimum(z, 0.0) - jnp.log1p(jnp.exp(-jnp.abs(z)))
        p1, p2, p3 = _split3(lf)
        cs = _dot(tri, p1) + _dot(tri, p2) + _dot(tri, p3) + carry
        cl = jnp.where(lane < N_HEADS, cs * LOG2E, 0.0)
        c1 = cl.astype(BF16).astype(F32)
        r1 = cl - c1
        c2 = r1.astype(BF16).astype(F32)
        c3 = r1 - c2
        out = c1 + pltpu.roll(c2, N_HEADS, axis=1) + pltpu.roll(c3, 2 * N_HEADS, axis=1)
        c_ref[pl.ds(r0, chunk), :] = out.astype(c_ref.dtype)
        return cs[chunk - 1:chunk, :]

    lax.fori_loop(0, L // chunk, body, jnp.zeros((1, LANE), F32))


def _foxgate(pb, fb_row, B, L):
    return pl.pallas_call(
        functools.partial(_foxgate_kernel, chunk=256),
        grid=(B,),
        in_specs=[pl.BlockSpec((L, LANE), lambda b: (b, GATE_BLOCK)),
                  pl.BlockSpec((1, LANE), lambda b: (0, 0))],
        out_specs=pl.BlockSpec((L, LANE), lambda b: (b, 0)),
        out_shape=jax.ShapeDtypeStruct((B * L, LANE), BF16),
        compiler_params=_cparams(("parallel",)),
        name="foxgate",
    )(pb, fb_row)


def _online_update(carry, s, vt):
    m, l, acc = carry
    m_new = jnp.maximum(m, jnp.max(s, axis=0, keepdims=True))
    alpha = jnp.exp2(m - m_new)
    p = jnp.exp2(s - m_new)
    l = alpha * l + jnp.sum(p, axis=0, keepdims=True)
    acc = alpha * acc + _dot(vt, p.astype(BF16))
    return m_new, l, acc


def _softmax_init(t):
    return (jnp.full((1, t), NEG, F32), jnp.zeros((1, t), F32), jnp.zeros((HEAD_DIM, t), F32))


def _causal(t):
    return lax.broadcasted_iota(I32, (t, t), 0) <= lax.broadcasted_iota(I32, (t, t), 1)


def _pair_rows(q_ref, h):
    p = h // 2
    return q_ref[p * LANE:(p + 1) * LANE, :]


def _q_spec(L, t, group):
    nq = L // t
    return pl.BlockSpec((GROUP, t), lambda b, i: (group, b * nq + i))


def _k_spec(L, block):
    return pl.BlockSpec((L, GROUP), lambda b, i: (b, block))


def _v_spec(L, t, group):
    return pl.BlockSpec((1, L // t, GROUP, t), lambda b, i: (group, b, 0, 0))


def _out_spec(L, t):
    nq = L // t
    return pl.BlockSpec((t, GROUP), lambda b, i: (b * nq + i, 0))


def _fox_kernel(q_ref, k_ref, ca_ref, v_ref, o_ref, qa_ref, *, t):
    i = pl.program_id(1)
    row = lax.broadcasted_iota(I32, (LANE, t), 0)
    for h in range(N_HEADS):
        qz = jnp.where((row // HEAD_DIM) == (h % 2), _pair_rows(q_ref, h), 0)
        sel = jnp.where((row == h) | (row == N_HEADS + h) | (row == 2 * N_HEADS + h), -1.0, 0.0).astype(BF16)
        qa_ref[h // 2, :, (h % 2) * t:(h % 2 + 1) * t] = jnp.concatenate([qz.astype(BF16), sel], axis=0)
    keep = _causal(t)

    def step(j, carry, masked):
        k0 = pl.multiple_of(j * t, t)
        kk = k_ref[pl.ds(k0, t), :]
        ca = ca_ref[pl.ds(k0, t), :]
        vt = v_ref[0, j]
        sp = [_dot(jnp.concatenate([kk[:, p * LANE:(p + 1) * LANE], ca], axis=1), qa_ref[p])
              for p in range(N_HEADS // 2)]
        out = []
        for h in range(N_HEADS):
            s = sp[h // 2][:, (h % 2) * t:(h % 2 + 1) * t]
            if masked:
                s = jnp.where(keep, s, NEG)
            out.append(_online_update(carry[h], s, vt[h * HEAD_DIM:(h + 1) * HEAD_DIM, :]))
        return tuple(out)

    carry = lax.fori_loop(0, i, functools.partial(step, masked=False),
                          tuple(_softmax_init(t) for _ in range(N_HEADS)))
    carry = step(i, carry, True)
    ot = jnp.concatenate([acc / l for (_, l, acc) in carry], axis=0)
    o_ref[...] = ot.T.astype(o_ref.dtype)


def _fox(qt, ka, caug, vt, B, L, t):
    return pl.pallas_call(
        functools.partial(_fox_kernel, t=t),
        grid=(B, L // t),
        in_specs=[_q_spec(L, t, QT_GROUP["f"]), _k_spec(L, KA_BLOCK["f"]),
                  pl.BlockSpec((L, LANE), lambda b, i: (b, 0)), _v_spec(L, t, 0)],
        out_specs=_out_spec(L, t),
        out_shape=jax.ShapeDtypeStruct((B * L, GROUP), BF16),
        scratch_shapes=[pltpu.VMEM((N_HEADS // 2, 2 * LANE, 2 * t), BF16)],
        compiler_params=_cparams(("parallel", "arbitrary")),
        name="fox",
    )(qt, ka, caug, vt)


def _diff_kernel(lam_ref, q_ref, k_ref, v_ref, dtab_ref, g_ref, o_ref, qz_ref, *, t, out_scale):
    i = pl.program_id(1)
    lam = lam_ref[0]
    row = lax.broadcasted_iota(I32, (LANE, t), 0)
    for h in range(N_HEADS):
        for m in range(2):
            lo = (h % 2) * HEAD_DIM + m * DIFF_QK
            n = 2 * (h % 2) + m
            qz_ref[h // 2, :, n * t:(n + 1) * t] = jnp.where((row >= lo) & (row < lo + DIFF_QK),
                                                            _pair_rows(q_ref, h), 0).astype(BF16)

    pieces = []
    for p in range(N_HEADS // 2):
        def step(j, carry, near, p=p):
            k0 = pl.multiple_of(j * t, t)
            vt = v_ref[0, j]
            sp = _dot(k_ref[pl.ds(k0, t), p * LANE:(p + 1) * LANE], qz_ref[p])
            out = []
            for hh in range(2):
                h = 2 * p + hh
                for m in range(2):
                    n = 2 * hh + m
                    s = sp[:, n * t:(n + 1) * t]
                    if near:
                        s = s + dtab_ref[h, i - j]
                    out.append(_online_update(carry[2 * hh + m], s, vt[h * HEAD_DIM:(h + 1) * HEAD_DIM, :]))
            return tuple(out)

        n_far = jnp.maximum(i - 1, 0)
        carry = lax.fori_loop(0, n_far, functools.partial(step, near=False),
                              tuple(_softmax_init(t) for _ in range(4)))
        carry = lax.fori_loop(n_far, i + 1, functools.partial(step, near=True), carry)
        for hh in range(2):
            (_, l1, a1), (_, l2, a2) = carry[2 * hh], carry[2 * hh + 1]
            o = a1 / l1 - lam * (a2 / l2)
            o = o * lax.rsqrt(jnp.mean(o * o, axis=0, keepdims=True) + LN_EPS)
            pieces.append(o * g_ref[...] * out_scale)
    o_ref[...] = jnp.concatenate(pieces, axis=0).T.astype(o_ref.dtype)


def _diff(lam, qt, ka, vt, dtab, g_tile, B, L, t, out_scale):
    return pl.pallas_call(
        functools.partial(_diff_kernel, t=t, out_scale=out_scale),
        grid=(B, L // t),
        in_specs=[pl.BlockSpec(memory_space=pltpu.SMEM),
                  _q_spec(L, t, QT_GROUP["d"]), _k_spec(L, KA_BLOCK["d"]), _v_spec(L, t, 1),
                  pl.BlockSpec((N_HEADS, 2, t, t), lambda b, i: (0, 0, 0, 0)),
                  pl.BlockSpec((HEAD_DIM, t), lambda b, i: (0, 0))],
        out_specs=_out_spec(L, t),
        out_shape=jax.ShapeDtypeStruct((B * L, GROUP), BF16),
        scratch_shapes=[pltpu.VMEM((N_HEADS // 2, LANE, 4 * t), BF16)],
        compiler_params=_cparams(("parallel", "arbitrary")),
        name="diff",
    )(lam, qt, ka, vt, dtab, g_tile)


def _floor_avg(a, b):
    return (a >> 1) + (b >> 1) + (a & b & 1)


def _dsa_kernel(q_ref, k_ref, v_ref, iq_ref, ik_ref, iw_ref, dtab_ref, o_ref, keys_ref, qz_ref, iqz_ref, gm_ref,
                *, t, top_k, L):
    i = pl.program_id(1)
    nblk = i + 1
    keep = _causal(t)
    key_row = lax.broadcasted_iota(I32, (t, t), 0)
    q_pos = i * t + lax.broadcasted_iota(I32, (1, t), 1)
    row = lax.broadcasted_iota(I32, (LANE, t), 0)
    for h in range(N_HEADS):
        qz_ref[h // 2, :, (h % 2) * t:(h % 2 + 1) * t] = jnp.where(
            (row // HEAD_DIM) == (h % 2), _pair_rows(q_ref, h), 0).astype(BF16)
        iqz_ref[:, h * t:(h + 1) * t] = jnp.concatenate(
            [iq_ref[h * D_IDX:(h + 1) * D_IDX, :], jnp.zeros((LANE - D_IDX, t), BF16)], axis=0)

    iw = iw_ref[...]

    gm_ref[...] = jnp.full((t, t), INT_MIN, I32)

    def score_block(c, _, masked):
        k0 = pl.multiple_of(c * t, t)
        a = _dot(ik_ref[pl.ds(k0, t), :], iqz_ref[...])
        acc = None
        for h in range(N_HEADS):
            term = jnp.maximum(a[:, h * t:(h + 1) * t], 0.0) * iw[h:h + 1, :]
            acc = term if acc is None else acc + term
        bits = lax.bitcast_convert_type(acc, I32)
        key = jnp.where(bits < 0, bits ^ INT_MAX, bits)
        key = key - (key >> 31)
        if masked:
            key = jnp.where(keep, key, INT_MIN)
        keys_ref[c] = key
        gm_ref[...] = jnp.maximum(gm_ref[...], key)
        return 0

    lax.fori_loop(0, i, functools.partial(score_block, masked=False), 0)
    score_block(i, 0, True)

    g = gm_ref[...]
    while g.shape[0] // 2 >= top_k:
        half = g.shape[0] // 2
        g = jnp.maximum(g[:half], g[half:])
    assert g.shape[0] >= top_k

    def fold_rows(x, op):
        return op(op(x.reshape(x.shape[0] // 8, 8, t), axis=0).astype(F32), axis=0, keepdims=True)

    margin, edge = 256.0, 2.0 ** 31 - 1024.0
    lo_b = jnp.maximum(fold_rows(g, jnp.min) - margin, -edge).astype(I32)
    hi_b = jnp.minimum(fold_rows(g, jnp.max) + margin, edge).astype(I32)

    def count(pred):
        def body(c, acc):
            return acc + jnp.sum(pred(keys_ref[c], c).reshape(t // 8, 8, t), axis=0)
        acc = lax.fori_loop(0, nblk, body, jnp.zeros((8, t), F32))
        return jnp.sum(acc, axis=0, keepdims=True).astype(I32)

    n_valid = q_pos + 1
    small = n_valid <= top_k
    lo0 = jnp.full((1, t), INT_MIN + 1, I32)
    state0 = dict(lo=lo0, hi=hi_b, clo=n_valid, chi=jnp.zeros((1, t), I32),
                  thr=lo0, need=jnp.zeros((1, t), I32), tied=jnp.zeros((1, t), I32), done=small.astype(I32))

    def n_active(done):
        return jnp.sum((1 - done).astype(F32))

    def body(c):
        _, st = c
        lo, hi, clo, chi, done = st["lo"], st["hi"], st["clo"], st["chi"], st["done"]
        mid = _floor_avg(lo, hi)
        mid = jnp.where((lo < 0) & (hi > 0), 0, mid)
        mid = jnp.where((lo == 0) & (hi > 1), 1, mid)
        first = (lo == INT_MIN + 1) & (lo_b > INT_MIN + 1)
        mid = jnp.where(first, jnp.minimum(lo_b, hi - 1), mid)
        cnt = count(lambda kc, _c: jnp.where(kc >= mid, 1.0, 0.0))
        act = done == 0
        ge = cnt >= top_k
        lo_n = jnp.where(ge, mid, lo)
        clo_n = jnp.where(ge, cnt, clo)
        hi_n = jnp.where(ge, hi, mid)
        chi_n = jnp.where(ge, chi, cnt)
        exact = cnt == top_k
        fin = act & (exact | (hi_n == lo_n + 1))
        upd = lambda new, old: jnp.where(act, new, old)
        st_n = dict(lo=upd(lo_n, lo), hi=upd(hi_n, hi), clo=upd(clo_n, clo), chi=upd(chi_n, chi),
                    thr=jnp.where(fin, jnp.where(exact, mid, lo_n), st["thr"]),
                    need=jnp.where(fin, top_k - chi_n, st["need"]),
                    tied=jnp.where(fin, ((~exact) & (clo_n > top_k)).astype(I32), st["tied"]),
                    done=jnp.where(fin, 1, done))
        return n_active(st_n["done"]), st_n

    _, st = lax.while_loop(lambda c: c[0] > 0.5, body, (n_active(state0["done"]), state0))
    thr, need, tied = st["thr"], st["need"], st["tied"] > 0

    @pl.when(jnp.sum(tied.astype(F32)) > 0.5)
    def _():
        tri = jnp.where(lax.broadcasted_iota(I32, (t, t), 1) <= key_row, 1.0, 0.0).astype(BF16)
        need_f = jnp.where(tied, need, 2 * L).astype(F32)

        def fbody(c, before):
            kc = keys_ref[c]
            e = jnp.where(kc == thr, 1.0, 0.0)
            rank = _dot(tri, e.astype(BF16)) + before
            keys_ref[c] = kc - jnp.where(rank > need_f, e, 0.0).astype(I32)
            return before + jnp.sum(e, axis=0, keepdims=True)

        lax.fori_loop(0, nblk, fbody, jnp.zeros((1, t), F32))

    def attn_block(c, carry, near):
        k0 = pl.multiple_of(c * t, t)
        kk = k_ref[pl.ds(k0, t), :]
        vt = v_ref[0, c]
        sel = keys_ref[c] >= thr
        sp = [_dot(kk[:, p * LANE:(p + 1) * LANE], qz_ref[p]) for p in range(N_HEADS // 2)]
        out = []
        for h in range(N_HEADS):
            s = sp[h // 2][:, (h % 2) * t:(h % 2 + 1) * t]
            if near:
                s = s + dtab_ref[h, i - c]
            s = jnp.where(sel, s, NEG)
            out.append(_online_update(carry[h], s, vt[h * HEAD_DIM:(h + 1) * HEAD_DIM, :]))
        return tuple(out)

    n_far = jnp.maximum(i - 1, 0)
    carry = lax.fori_loop(0, n_far, functools.partial(attn_block, near=False),
                          tuple(_softmax_init(t) for _ in range(N_HEADS)))
    carry = lax.fori_loop(n_far, nblk, functools.partial(attn_block, near=True), carry)
    ot = jnp.concatenate([acc / l for (_, l, acc) in carry], axis=0)
    o_ref[...] = ot.T.astype(o_ref.dtype)


def _dsa(qt, ka, vt, gt, dtab, B, L, t, top_k):
    nq = L // t
    return pl.pallas_call(
        functools.partial(_dsa_kernel, t=t, top_k=top_k, L=L),
        grid=(B, nq),
        in_specs=[_q_spec(L, t, QT_GROUP["s"]), _k_spec(L, KA_BLOCK["s"]), _v_spec(L, t, 2),
                  _q_spec(L, t, QT_GROUP["i"]),
                  pl.BlockSpec((L, LANE), lambda b, i: (b, IK_BLOCK)),
                  pl.BlockSpec((GT_ROWS, t), lambda b, i: (0, b * nq + i)),
                  pl.BlockSpec((N_HEADS, 2, t, t), lambda b, i: (0, 0, 0, 0))],
        out_specs=_out_spec(L, t),
        out_shape=jax.ShapeDtypeStruct((B * L, GROUP), BF16),
        scratch_shapes=[pltpu.VMEM((nq, t, t), I32), pltpu.VMEM((N_HEADS // 2, LANE, 2 * t), BF16),
                        pltpu.VMEM((LANE, N_HEADS * t), BF16), pltpu.VMEM((t, t), I32)],
        compiler_params=_cparams(("parallel", "arbitrary")),
        name="dsa",
    )(qt, ka, vt, qt, ka, gt, dtab)


def _conv_kernel(u_ref, halo_ref, w_ref, cb_ref, g_ref, b_ref, o_ref, h_ref, *, tc):
    i = pl.program_id(1)

    def glu(u):
        return u[:, :GROUP] * jax.nn.sigmoid(u[:, GROUP:])

    h_ref[pl.ds(CONV_HALO, tc), :] = glu(u_ref[...])
    h_ref[pl.ds(0, CONV_HALO), :] = jnp.where(i > 0, glu(halo_ref[...]), 0.0)
    acc = jnp.zeros((tc, GROUP), F32)
    for j in range(CONV_W):
        acc = acc + h_ref[pl.ds(CONV_HALO - (CONV_W - 1) + j, tc), :] * w_ref[j:j + 1, :]
    y = _layernorm(acc + cb_ref[...], g_ref[...], b_ref[...])
    o_ref[...] = (y * jax.nn.sigmoid(y)).astype(o_ref.dtype)


def _conv(pb, conv_w, cb, g, b, B, L, tc):
    nc = L // tc
    r = tc // CONV_HALO
    row = lambda a: a.reshape(1, GROUP)
    return pl.pallas_call(
        functools.partial(_conv_kernel, tc=tc),
        grid=(B, nc),
        in_specs=[pl.BlockSpec((tc, 2 * GROUP), lambda bb, i: (bb * nc + i, 0)),
                  pl.BlockSpec((CONV_HALO, 2 * GROUP), lambda bb, i: (jnp.maximum((bb * nc + i) * r - 1, 0), 0)),
                  pl.BlockSpec((CONV_W, GROUP), lambda bb, i: (0, 0)),
                  pl.BlockSpec((1, GROUP), lambda bb, i: (0, 0)),
                  pl.BlockSpec((1, GROUP), lambda bb, i: (0, 0)),
                  pl.BlockSpec((1, GROUP), lambda bb, i: (0, 0))],
        out_specs=pl.BlockSpec((tc, GROUP), lambda bb, i: (bb * nc + i, 0)),
        out_shape=jax.ShapeDtypeStruct((B * L, GROUP), BF16),
        scratch_shapes=[pltpu.VMEM((tc + CONV_HALO, GROUP), F32)],
        compiler_params=_cparams(("parallel", "arbitrary")),
        name="conv",
    )(pb, pb, conv_w, row(cb), row(g), row(b))


def _mix_kernel(yf_ref, yd_ref, ys_ref, yc_ref, x_ref, wo_ref, g_ref, b_ref, rw_ref, rb_ref,
                x1_ref, idx_ref, wt_ref, *, alpha):
    mix = None
    for n, y_ref in enumerate((yf_ref, yd_ref, ys_ref, yc_ref)):
        part = _dot(y_ref[...], wo_ref[n * GROUP:(n + 1) * GROUP, :])
        mix = part if mix is None else mix + part
    x1 = _layernorm(alpha * x_ref[...] + mix, g_ref[...], b_ref[...])
    x1_ref[...] = x1

    a1, a2, a3 = _split3(x1)
    w1, w2, w3 = rw_ref[0], rw_ref[1], rw_ref[2]
    lg = (_dot(a1, w1) + (_dot(a1, w2) + _dot(a2, w1)) + (_dot(a1, w3) + _dot(a2, w2) + _dot(a3, w1))) + rb_ref[...]

    tm = lg.shape[0]
    lane = lax.broadcasted_iota(I32, (tm, LANE), 1)
    lane_f = lane.astype(F32)
    idx_out = jnp.zeros((tm, LANE), I32)
    vals = []
    for r in range(TOP_K):
        mx = jnp.max(lg, axis=1, keepdims=True)
        am = jnp.min(jnp.where(lg == mx, lane_f, float(LANE)), axis=1, keepdims=True).astype(I32)
        vals.append(mx)
        idx_out = jnp.where(lane == r, am, idx_out)
        lg = jnp.where(lane == am, 2 * NEG, lg)
    es = [jnp.exp(v - vals[0]) for v in vals]
    den = es[0] + es[1] + es[2] + es[3]
    wt = jnp.zeros((tm, LANE), F32)
    for r in range(TOP_K):
        wt = jnp.where(lane == r, es[r] / den, wt)
    idx_ref[...] = idx_out
    wt_ref[...] = wt


def _mix(ys, x2d, wo, g, b, rw3, rb, alpha, tm):
    T = x2d.shape[0]
    yspec = pl.BlockSpec((tm, GROUP), lambda i: (i, 0))
    full = lambda shp: pl.BlockSpec(shp, lambda i: (0,) * len(shp))
    return pl.pallas_call(
        functools.partial(_mix_kernel, alpha=alpha),
        grid=(T // tm,),
        in_specs=[yspec, yspec, yspec, yspec,
                  pl.BlockSpec((tm, D_MODEL), lambda i: (i, 0)),
                  full((D_MODEL, D_MODEL)), full((1, D_MODEL)), full((1, D_MODEL)),
                  full((3, D_MODEL, LANE)), full((1, LANE))],
        out_specs=[pl.BlockSpec((tm, D_MODEL), lambda i: (i, 0)),
                   pl.BlockSpec((tm, LANE), lambda i: (i, 0)),
                   pl.BlockSpec((tm, LANE), lambda i: (i, 0))],
        out_shape=[jax.ShapeDtypeStruct((T, D_MODEL), F32),
                   jax.ShapeDtypeStruct((T, LANE), I32),
                   jax.ShapeDtypeStruct((T, LANE), F32)],
        compiler_params=_cparams(("parallel",)),
        name="mix",
    )(*ys, x2d, wo, g.reshape(1, D_MODEL), b.reshape(1, D_MODEL), rw3, rb)


def _start_row_gather(idx_ref, base, n, src_hbm, dst_ref, sem, inline=False):
    def body(r, _):
        pltpu.make_async_copy(src_hbm.at[pl.ds(idx_ref[base + r], 1)], dst_ref.at[pl.ds(r, 1)], sem).start()
        return 0
    if inline:
        for r in range(n):
            body(r, 0)
    else:
        lax.fori_loop(0, n, body, 0, unroll=8)


def _wait_row_gather(n, src_hbm, dst_ref, sem):
    pltpu.make_async_copy(src_hbm.at[pl.ds(0, n)], dst_ref, sem).wait()


def _expert_kernel(te_ref, tv_ref, tb_ref, tok_ref, x_hbm, wt_ref, bg_ref, bu_ref, wd_ref, bd_ref, y_ref,
                   xbuf, sem, wg_s, wu_s, wd_s, *, tmm):
    i = pl.program_id(0)
    n = pl.num_programs(0)
    slot = lax.rem(i, 2)

    @pl.when(i == 0)
    def _():
        _start_row_gather(tok_ref, tb_ref[0], tmm, x_hbm, xbuf.at[0], sem.at[0])

    _wait_row_gather(tmm, x_hbm, xbuf.at[slot], sem.at[slot])
    prefetch = functools.partial(_start_row_gather, tok_ref, tb_ref[i + 1], tmm, x_hbm, xbuf.at[1 - slot],
                                 sem.at[1 - slot])

    @pl.when((tv_ref[i] > 0) & ((i == 0) | (te_ref[i] != te_ref[jnp.maximum(i - 1, 0)])))
    def _():
        for c in range(D_MODEL // LANE):
            rows = slice(c * LANE, (c + 1) * LANE)
            wg_s[rows, :] = wt_ref[0, c, pl.ds(0, D_MODEL, stride=2), :].T.astype(BF16)
            wu_s[rows, :] = wt_ref[0, c, pl.ds(1, D_MODEL, stride=2), :].T.astype(BF16)
        wd_s[...] = wd_ref[0].astype(BF16)

    @pl.when(tv_ref[i] > 0)
    def _():
        prefetch(inline=True)
        xb = xbuf[slot].astype(BF16)
        gate = jnp.minimum(_dot(xb, wg_s[...]) + bg_ref[0], SWIGLU_LIMIT)
        up = jnp.clip(_dot(xb, wu_s[...]) + bu_ref[0], -SWIGLU_LIMIT, SWIGLU_LIMIT)
        h = (up + 1.0) * gate * jax.nn.sigmoid(gate * SWIGLU_ALPHA)
        y_ref[...] = _dot(h.astype(BF16), wd_s[...]) + bd_ref[0]

    @pl.when(tv_ref[i] == 0)
    def _():
        prefetch()
        y_ref[...] = jnp.zeros_like(y_ref)

    @pl.when(i == n - 1)
    def _():
        _wait_row_gather(tmm, x_hbm, xbuf.at[1 - slot], sem.at[1 - slot])


def _experts(tile_e, tile_v, tile_base, tok, x1, wgu_t, bg, bu, wd, bd, R, tmm):
    by_expert = lambda shp: pl.BlockSpec((1,) + shp, lambda i, te, tv, tb, tk: (te[i],) + (0,) * len(shp))
    wscratch = pltpu.VMEM((D_MODEL, D_MODEL), BF16)
    return pl.pallas_call(
        functools.partial(_expert_kernel, tmm=tmm),
        grid_spec=pltpu.PrefetchScalarGridSpec(
            num_scalar_prefetch=4,
            grid=(R // tmm,),
            in_specs=[pl.BlockSpec(memory_space=pl.ANY),
                      by_expert((D_MODEL // LANE, 2 * D_MODEL, LANE)), by_expert((1, D_MODEL)),
                      by_expert((1, D_MODEL)),
                      by_expert((D_MODEL, D_MODEL)), by_expert((1, D_MODEL))],
            out_specs=pl.BlockSpec((tmm, D_MODEL), lambda i, te, tv, tb, tk: (i, 0)),
            scratch_shapes=[pltpu.VMEM((2, tmm, D_MODEL), F32), pltpu.SemaphoreType.DMA((2,)),
                            wscratch, wscratch, wscratch],
        ),
        out_shape=jax.ShapeDtypeStruct((R, D_MODEL), F32),
        compiler_params=_cparams(("arbitrary",)),
        name="moe_experts",
    )(tile_e, tile_v, tile_base, tok, x1, wgu_t, bg, bu, wd, bd)


def _combine_kernel(pos_ref, ys_hbm, x1_ref, wt_ref, g_ref, b_ref, o_ref, buf, sem, *, tc, alpha):
    i = pl.program_id(0)
    n = pl.num_programs(0)
    slot = lax.rem(i, 2)
    rows = tc * TOP_K

    @pl.when(i == 0)
    def _():
        _start_row_gather(pos_ref, 0, rows, ys_hbm, buf.at[0], sem.at[0])

    @pl.when(i + 1 < n)
    def _():
        _start_row_gather(pos_ref, (i + 1) * rows, rows, ys_hbm, buf.at[1 - slot], sem.at[1 - slot], inline=True)

    _wait_row_gather(rows, ys_hbm, buf.at[slot], sem.at[slot])
    wt = wt_ref[...]
    ffn = buf[slot, pl.ds(0, tc), :] * wt[:, 0:1]
    for k in range(1, TOP_K):
        ffn = ffn + buf[slot, pl.ds(k * tc, tc), :] * wt[:, k:k + 1]
    o_ref[...] = _layernorm(alpha * x1_ref[...] + ffn, g_ref[...], b_ref[...])


def _combine(pos, ys, x1, wt, g, b, alpha, tc):
    T = x1.shape[0]
    pos_t = jnp.transpose(pos.reshape(T // tc, tc, TOP_K), (0, 2, 1)).reshape(T * TOP_K)
    return pl.pallas_call(
        functools.partial(_combine_kernel, tc=tc, alpha=alpha),
        grid_spec=pltpu.PrefetchScalarGridSpec(
            num_scalar_prefetch=1,
            grid=(T // tc,),
            in_specs=[pl.BlockSpec(memory_space=pl.ANY),
                      pl.BlockSpec((tc, D_MODEL), lambda i, p: (i, 0)),
                      pl.BlockSpec((tc, LANE), lambda i, p: (i, 0)),
                      pl.BlockSpec((1, D_MODEL), lambda i, p: (0, 0)),
                      pl.BlockSpec((1, D_MODEL), lambda i, p: (0, 0))],
            out_specs=pl.BlockSpec((tc, D_MODEL), lambda i, p: (i, 0)),
            scratch_shapes=[pltpu.VMEM((2, TOP_K * tc, D_MODEL), F32), pltpu.SemaphoreType.DMA((2,))],
        ),
        out_shape=jax.ShapeDtypeStruct((T, D_MODEL), F32),
        compiler_params=_cparams(("arbitrary",)),
        name="moe_combine",
    )(pos_t, ys, x1, wt, g.reshape(1, D_MODEL), b.reshape(1, D_MODEL))


def _moe_plan(top_idx, tmm):
    T = top_idx.shape[0]
    F = T * TOP_K
    R = F + N_EXPERTS * tmm
    flat_e = top_idx.reshape(F)
    onehot = (flat_e[:, None] == jnp.arange(N_EXPERTS, dtype=I32)[None, :]).astype(I32)
    csum = jnp.cumsum(onehot, axis=0)
    counts = csum[-1]
    padded = ((counts + tmm - 1) // tmm) * tmm
    gend = jnp.cumsum(padded)
    gstart = gend - padded
    ustart = jnp.cumsum(counts) - counts
    pos = jnp.sum(onehot * (csum - 1 + gstart[None, :]), axis=1)
    order = jnp.argsort(flat_e, stable=True).astype(I32)
    n_tiles = R // tmm + 1
    t_start = jnp.arange(n_tiles, dtype=I32) * tmm
    t_e = jnp.minimum(jnp.sum((t_start[:, None] >= gend[None, :]).astype(I32), axis=1), N_EXPERTS - 1)
    tile_base = jnp.clip(ustart[t_e] + t_start - gstart[t_e], 0, F)
    tok = jnp.concatenate([order // TOP_K, jnp.zeros((tmm,), I32)])
    tile_v = (t_start[:-1] < gend[-1]).astype(I32)
    last_e = jnp.sum((gend[-1] - 1 >= gend).astype(I32))
    tile_e = jnp.where(tile_v > 0, t_e[:-1], last_e)
    return pos.astype(I32), tok, tile_base.astype(I32), tile_e.astype(I32), tile_v, R


def _rel_bucket(dist):
    n = jnp.maximum(dist, 0)
    max_exact = NUM_BUCKETS // 2
    nf = jnp.maximum(n, 1).astype(F32)
    large = max_exact + (jnp.log(nf / max_exact) / math.log(MAX_DISTANCE / max_exact)
                         * (NUM_BUCKETS - max_exact)).astype(I32)
    large = jnp.minimum(large, NUM_BUCKETS - 1)
    return jnp.where(n < max_exact, n, large)


def _bias_tables(table, t):
    assert t >= MAX_DISTANCE, "key blocks two or more away must lie entirely in the last bucket"
    d0 = jnp.arange(t, dtype=I32)[None, :] - jnp.arange(t, dtype=I32)[:, None]
    rel = (table - table[NUM_BUCKETS - 1][None, :]) * LOG2E
    out = []
    for d in (d0, d0 + t):
        b = jnp.transpose(rel[_rel_bucket(d)], (2, 0, 1))
        out.append(jnp.where((d >= 0)[None], b, NEG))
    return jnp.stack(out, axis=1).astype(F32)


def kernel(x, rel_bias, w_in, forget_b, diff_lambda, diff_norm_g, conv_w, conv_b, conv_ln_g, conv_ln_b,
           w_out, ln1_g, ln1_b, router_w, router_b, w_gu, b_gu, w_down, b_down, ln2_g, ln2_b):
    B, L, D = x.shape
    depth = w_in.shape[0]
    assert D == D_MODEL
    T = B * L
    t = 512 if L % 512 == 0 else 256
    tm = 256
    tmm = 256
    tcv = 512 if L % 512 == 0 else 256
    assert L % t == 0 and T % tm == 0
    top_k = min(K_SEL_MAX, L // 4)
    alpha = (2 * depth) ** 0.25

    dtab_diff = _bias_tables(rel_bias[:, :N_HEADS], t)
    dtab_dsa = _bias_tables(rel_bias[:, N_HEADS:], t)

    x2d = x.reshape(T, D)
    for l in range(depth):
        lambda_init = 0.8 - 0.6 * math.exp(-0.3 * l)
        qt, ka, vt, pb, gt = _proj(x2d, _proj_weights(w_in[l]), t)

        fb_row = jnp.zeros((1, LANE), F32).at[0, :N_HEADS].set(forget_b[l])
        caug = _foxgate(pb, fb_row, B, L)
        y_fox = _fox(qt, ka, caug, vt, B, L, t)

        lp = diff_lambda[l].astype(F32)
        lam = (jnp.exp(jnp.sum(lp[0] * lp[1])) - jnp.exp(jnp.sum(lp[2] * lp[3])) + lambda_init).reshape(1)
        g_tile = jnp.broadcast_to(diff_norm_g[l].astype(F32)[:, None], (HEAD_DIM, t))
        y_diff = _diff(lam, qt, ka, vt, dtab_diff, g_tile, B, L, t, 1.0 - lambda_init)

        y_dsa = _dsa(qt, ka, vt, gt, dtab_dsa, B, L, t, top_k)
        y_conv = _conv(pb, conv_w[l], conv_b[l], conv_ln_g[l], conv_ln_b[l], B, L, tcv)

        rw = jnp.zeros((D, LANE), F32).at[:, :N_EXPERTS].set(router_w[l])
        r1 = rw.astype(BF16)
        r2 = (rw - r1.astype(F32)).astype(BF16)
        r3 = (rw - r1.astype(F32) - r2.astype(F32)).astype(BF16)
        rb = jnp.full((1, LANE), NEG, F32).at[0, :N_EXPERTS].set(router_b[l])
        x1, top_idx, top_w = _mix((y_fox, y_diff, y_dsa, y_conv), x2d, w_out[l].astype(BF16),
                                  ln1_g[l], ln1_b[l], jnp.stack([r1, r2, r3]), rb, alpha, tm)

        pos, tok, tile_base, tile_e, tile_v, R = _moe_plan(top_idx[:, :TOP_K], tmm)
        wgu_t = jnp.transpose(w_gu[l].reshape(N_EXPERTS, D // LANE, LANE, 2 * D), (0, 1, 3, 2))
        ys = _experts(tile_e, tile_v, tile_base, tok, x1, wgu_t, b_gu[l][:, None, 0::2],
                      b_gu[l][:, None, 1::2], w_down[l], b_down[l][:, None, :], R, tmm)
        x2d = _combine(pos, ys, x1, top_w, ln2_g[l], ln2_b[l], alpha, 128)
    return x2d.reshape(B, L, D)
```

```python
import functools
import math

import numpy as np
import jax
import jax.numpy as jnp
from jax import lax
from jax.experimental import pallas as pl
from jax.experimental.pallas import tpu as pltpu

F32 = jnp.float32
BF16 = jnp.bfloat16
I32 = jnp.int32

D_MODEL = 1024
HEAD_DIM = 64
N_HEADS = 4
GROUP = N_HEADS * HEAD_DIM
DIFF_QK = HEAD_DIM // 2
D_IDX = 64
K_SEL_MAX = 256
CONV_W = 31
CONV_HALO = 32
NUM_BUCKETS = 32
MAX_DISTANCE = 128
N_EXPERTS = 32
TOP_K = 4
SWIGLU_LIMIT = 7.0
SWIGLU_ALPHA = 1.702
LN_EPS = 1e-5
LOG2E = math.log2(math.e)
NEG = -1e30
INT_MIN = -(2 ** 31)
INT_MAX = 2 ** 31 - 1
LANE = 128
VMEM_LIMIT = 56 * 1024 * 1024

_IN_WIDTHS = (GROUP, GROUP, GROUP, N_HEADS, GROUP, GROUP, GROUP, GROUP, GROUP, GROUP,
              N_HEADS * D_IDX, D_IDX, N_HEADS, 2 * GROUP)
_IN_OFF = tuple(int(v) for v in np.cumsum((0,) + _IN_WIDTHS))
(_FQ, _FK, _FV, _FF, _DQ, _DK, _DV, _SQ, _SK, _SV, _IQ, _IK, _IW, _CU) = range(14)

QT_GROUP = {"f": 0, "d": 1, "s": 2, "i": 3}
KA_BLOCK = {"f": 0, "d": 1, "s": 2}
NK = 3 * GROUP + LANE
IK_BLOCK = 3 * GROUP // LANE
NB = 2 * GROUP + LANE
GATE_BLOCK = 2 * GROUP // LANE
GT_ROWS = 16


def _cols(w, slots, scales=None):
    parts = []
    for n, s in enumerate(slots):
        part = w[:, _IN_OFF[s]:_IN_OFF[s + 1]]
        parts.append(part if scales is None or scales[n] == 1.0 else part * scales[n])
    return jnp.concatenate(parts, axis=1)


def _proj_weights(w):
    qs = HEAD_DIM ** -0.5 * LOG2E
    iw_scale = N_HEADS ** -0.5
    wq_t = _cols(w, (_FQ, _DQ, _SQ, _IQ), (qs, DIFF_QK ** -0.5 * LOG2E, qs, D_IDX ** -0.5)).T.astype(BF16)
    wk = jnp.pad(_cols(w, (_FK, _DK, _SK, _IK)), ((0, 0), (0, LANE - D_IDX))).astype(BF16)
    wv_t = _cols(w, (_FV, _DV, _SV)).T.astype(BF16)
    wb = jnp.pad(_cols(w, (_CU, _FF, _IW), (1.0, 1.0, iw_scale)),
                 ((0, 0), (0, LANE - 2 * N_HEADS))).astype(BF16)
    wg_t = jnp.pad((w[:, _IN_OFF[_IW]:_IN_OFF[_IW + 1]] * iw_scale).T,
                   ((0, GT_ROWS - N_HEADS), (0, 0))).astype(BF16)
    return wq_t, wk, wv_t, wb, wg_t


def _cparams(sem):
    return pltpu.CompilerParams(dimension_semantics=sem, vmem_limit_bytes=VMEM_LIMIT)


def _dot(a, b):
    return jnp.dot(a, b, preferred_element_type=F32)


def _dot_nt(a, b):
    return lax.dot_general(a, b, (((1,), (1,)), ((), ())), preferred_element_type=F32)


def _split3(a):
    a1 = a.astype(BF16)
    r1 = a - a1.astype(F32)
    a2 = r1.astype(BF16)
    a3 = (r1 - a2.astype(F32)).astype(BF16)
    return a1, a2, a3


def _layernorm(z, g, b):
    mu = jnp.mean(z, axis=-1, keepdims=True)
    zc = z - mu
    var = jnp.mean(zc * zc, axis=-1, keepdims=True)
    return zc * lax.rsqrt(var + LN_EPS) * g + b


def _proj_kernel(x_ref, wq_ref, wk_ref, wv_ref, wb_ref, wg_ref, qt_ref, ka_ref, vt_ref, pb_ref, gt_ref):
    xb = x_ref[...].astype(BF16)
    qt_ref[...] = _dot_nt(wq_ref[...], xb).astype(qt_ref.dtype)
    ka_ref[...] = _dot(xb, wk_ref[...]).astype(ka_ref.dtype)
    vt = _dot_nt(wv_ref[...], xb)
    for g in range(3):
        vt_ref[g, 0] = vt[g * GROUP:(g + 1) * GROUP, :].astype(vt_ref.dtype)
    pb_ref[...] = _dot(xb, wb_ref[...])
    gt_ref[...] = _dot_nt(wg_ref[...], xb)


def _proj(x2d, weights, t):
    T = x2d.shape[0]
    full = lambda a: pl.BlockSpec(a.shape, lambda i: (0,) * a.ndim)
    return pl.pallas_call(
        _proj_kernel,
        grid=(T // t,),
        in_specs=[pl.BlockSpec((t, D_MODEL), lambda i: (i, 0))] + [full(a) for a in weights],
        out_specs=[pl.BlockSpec((4 * GROUP, t), lambda i: (0, i)),
                   pl.BlockSpec((t, NK), lambda i: (i, 0)),
                   pl.BlockSpec((3, 1, GROUP, t), lambda i: (0, i, 0, 0)),
                   pl.BlockSpec((t, NB), lambda i: (i, 0)),
                   pl.BlockSpec((GT_ROWS, t), lambda i: (0, i))],
        out_shape=[jax.ShapeDtypeStruct((4 * GROUP, T), BF16),
                   jax.ShapeDtypeStruct((T, NK), BF16),
                   jax.ShapeDtypeStruct((3, T // t, GROUP, t), BF16),
                   jax.ShapeDtypeStruct((T, NB), F32),
                   jax.ShapeDtypeStruct((GT_ROWS, T), F32)],
        compiler_params=_cparams(("parallel",)),
        name="proj",
    )(x2d, *weights)


def _foxgate_kernel(g_ref, fb_ref, c_ref, *, chunk):
    L = g_ref.shape[0]
    row = lax.broadcasted_iota(I32, (chunk, chunk), 0)
    col = lax.broadcasted_iota(I32, (chunk, chunk), 1)
    tri = jnp.where(col <= row, 1.0, 0.0).astype(BF16)
    lane = lax.broadcasted_iota(I32, (chunk, LANE), 1)

    def body(i, carry):
        r0 = pl.multiple_of(i * chunk, chunk)
        z = g_ref[pl.ds(r0, chunk), :] + fb_ref[...]
        lf = jnp.minimum(z, 0.0) - jnp.log1p(jnp.exp(-jnp.abs(z)))
        p1, p2, p3 = _split3(lf)
        cs = _dot(tri, p1) + _dot(tri, p2) + _dot(tri, p3) + carry
        cl = jnp.where(lane < N_HEADS, cs * LOG2E, 0.0)
        c1 = cl.astype(BF16).astype(F32)
        r1 = cl - c1
        c2 = r1.astype(BF16).astype(F32)
        c3 = r1 - c2
        out = c1 + pltpu.roll(c2, N_HEADS, axis=1) + pltpu.roll(c3, 2 * N_HEADS, axis=1)
        c_ref[pl.ds(r0, chunk), :] = out.astype(c_ref.dtype)
        return cs[chunk - 1:chunk, :]

    lax.fori_loop(0, L // chunk, body, jnp.zeros((1, LANE), F32))


def _foxgate(pb, fb_row, B, L):
    return pl.pallas_call(
        functools.partial(_foxgate_kernel, chunk=256),
        grid=(B,),
        in_specs=[pl.BlockSpec((L, LANE), lambda b: (b, GATE_BLOCK)),
                  pl.BlockSpec((1, LANE), lambda b: (0, 0))],
        out_specs=pl.BlockSpec((L, LANE), lambda b: (b, 0)),
        out_shape=jax.ShapeDtypeStruct((B * L, LANE), BF16),
        compiler_params=_cparams(("parallel",)),
        name="foxgate",
    )(pb, fb_row)


def _online_update(carry, s, vt):
    m, l, acc = carry
    m_new = jnp.maximum(m, jnp.max(s, axis=0, keepdims=True))
    alpha = jnp.exp2(m - m_new)
    p = jnp.exp2(s - m_new)
    l = alpha * l + jnp.sum(p, axis=0, keepdims=True)
    acc = alpha * acc + _dot(vt, p.astype(BF16))
    return m_new, l, acc


def _softmax_init(t):
    return (jnp.full((1, t), NEG, F32), jnp.zeros((1, t), F32), jnp.zeros((HEAD_DIM, t), F32))


def _causal(t):
    return lax.broadcasted_iota(I32, (t, t), 0) <= lax.broadcasted_iota(I32, (t, t), 1)


def _pair_rows(q_ref, h):
    p = h // 2
    return q_ref[p * LANE:(p + 1) * LANE, :]


def _q_spec(L, t, group):
    nq = L // t
    return pl.BlockSpec((GROUP, t), lambda b, i: (group, b * nq + i))


def _k_spec(L, block):
    return pl.BlockSpec((L, GROUP), lambda b, i: (b, block))


def _v_spec(L, t, group):
    return pl.BlockSpec((1, L // t, GROUP, t), lambda b, i: (group, b, 0, 0))


def _out_spec(L, t):
    nq = L // t
    return pl.BlockSpec((t, GROUP), lambda b, i: (b * nq + i, 0))


def _fox_kernel(q_ref, k_ref, ca_ref, v_ref, o_ref, qa_ref, *, t):
    i = pl.program_id(1)
    row = lax.broadcasted_iota(I32, (LANE, t), 0)
    for h in range(N_HEADS):
        qz = jnp.where((row // HEAD_DIM) == (h % 2), _pair_rows(q_ref, h), 0)
        sel = jnp.where((row == h) | (row == N_HEADS + h) | (row == 2 * N_HEADS + h), -1.0, 0.0).astype(BF16)
        qa_ref[h // 2, :, (h % 2) * t:(h % 2 + 1) * t] = jnp.concatenate([qz.astype(BF16), sel], axis=0)
    keep = _causal(t)

    def step(j, carry, masked):
        k0 = pl.multiple_of(j * t, t)
        kk = k_ref[pl.ds(k0, t), :]
        ca = ca_ref[pl.ds(k0, t), :]
        vt = v_ref[0, j]
        sp = [_dot(jnp.concatenate([kk[:, p * LANE:(p + 1) * LANE], ca], axis=1), qa_ref[p])
              for p in range(N_HEADS // 2)]
        out = []
        for h in range(N_HEADS):
            s = sp[h // 2][:, (h % 2) * t:(h % 2 + 1) * t]
            if masked:
                s = jnp.where(keep, s, NEG)
            out.append(_online_update(carry[h], s, vt[h * HEAD_DIM:(h + 1) * HEAD_DIM, :]))
        return tuple(out)

    carry = lax.fori_loop(0, i, functools.partial(step, masked=False),
                          tuple(_softmax_init(t) for _ in range(N_HEADS)))
    carry = step(i, carry, True)
    ot = jnp.concatenate([acc / l for (_, l, acc) in carry], axis=0)
    o_ref[...] = ot.T.astype(o_ref.dtype)


def _fox(qt, ka, caug, vt, B, L, t):
    return pl.pallas_call(
        functools.partial(_fox_kernel, t=t),
        grid=(B, L // t),
        in_specs=[_q_spec(L, t, QT_GROUP["f"]), _k_spec(L, KA_BLOCK["f"]),
                  pl.BlockSpec((L, LANE), lambda b, i: (b, 0)), _v_spec(L, t, 0)],
        out_specs=_out_spec(L, t),
        out_shape=jax.ShapeDtypeStruct((B * L, GROUP), BF16),
        scratch_shapes=[pltpu.VMEM((N_HEADS // 2, 2 * LANE, 2 * t), BF16)],
        compiler_params=_cparams(("parallel", "arbitrary")),
        name="fox",
    )(qt, ka, caug, vt)


def _diff_kernel(lam_ref, q_ref, k_ref, v_ref, dtab_ref, g_ref, o_ref, qz_ref, *, t, out_scale):
    i = pl.program_id(1)
    lam = lam_ref[0]
    row = lax.broadcasted_iota(I32, (LANE, t), 0)
    for h in range(N_HEADS):
        for m in range(2):
            lo = (h % 2) * HEAD_DIM + m * DIFF_QK
            n = 2 * (h % 2) + m
            qz_ref[h // 2, :, n * t:(n + 1) * t] = jnp.where((row >= lo) & (row < lo + DIFF_QK),
                                                            _pair_rows(q_ref, h), 0).astype(BF16)

    pieces = []
    for p in range(N_HEADS // 2):
        def step(j, carry, near, p=p):
            k0 = pl.multiple_of(j * t, t)
            vt = v_ref[0, j]
            sp = _dot(k_ref[pl.ds(k0, t), p * LANE:(p + 1) * LANE], qz_ref[p])
            out = []
            for hh in range(2):
                h = 2 * p + hh
                for m in range(2):
                    n = 2 * hh + m
                    s = sp[:, n * t:(n + 1) * t]
                    if near:
                        s = s + dtab_ref[h, i - j]
                    out.append(_online_update(carry[2 * hh + m], s, vt[h * HEAD_DIM:(h + 1) * HEAD_DIM, :]))
            return tuple(out)

        n_far = jnp.maximum(i - 1, 0)
        carry = lax.fori_loop(0, n_far, functools.partial(step, near=False),
                              tuple(_softmax_init(t) for _ in range(4)))
        carry = lax.fori_loop(n_far, i + 1, functools.partial(step, near=True), carry)
        for hh in range(2):
            (_, l1, a1), (_, l2, a2) = carry[2 * hh], carry[2 * hh + 1]
            o = a1 / l1 - lam * (a2 / l2)
            o = o * lax.rsqrt(jnp.mean(o * o, axis=0, keepdims=True) + LN_EPS)
            pieces.append(o * g_ref[...] * out_scale)
    o_ref[...] = jnp.concatenate(pieces, axis=0).T.astype(o_ref.dtype)


def _diff(lam, qt, ka, vt, dtab, g_tile, B, L, t, out_scale):
    return pl.pallas_call(
        functools.partial(_diff_kernel, t=t, out_scale=out_scale),
        grid=(B, L // t),
        in_specs=[pl.BlockSpec(memory_space=pltpu.SMEM),
                  _q_spec(L, t, QT_GROUP["d"]), _k_spec(L, KA_BLOCK["d"]), _v_spec(L, t, 1),
                  pl.BlockSpec((N_HEADS, 2, t, t), lambda b, i: (0, 0, 0, 0)),
                  pl.BlockSpec((HEAD_DIM, t), lambda b, i: (0, 0))],
        out_specs=_out_spec(L, t),
        out_shape=jax.ShapeDtypeStruct((B * L, GROUP), BF16),
        scratch_shapes=[pltpu.VMEM((N_HEADS // 2, LANE, 4 * t), BF16)],
        compiler_params=_cparams(("parallel", "arbitrary")),
        name="diff",
    )(lam, qt, ka, vt, dtab, g_tile)


def _floor_avg(a, b):
    return (a >> 1) + (b >> 1) + (a & b & 1)


def _dsa_kernel(q_ref, k_ref, v_ref, iq_ref, ik_ref, iw_ref, dtab_ref, o_ref, keys_ref, qz_ref, iqz_ref, gm_ref,
                *, t, top_k, L):
    i = pl.program_id(1)
    nblk = i + 1
    keep = _causal(t)
    key_row = lax.broadcasted_iota(I32, (t, t), 0)
    q_pos = i * t + lax.broadcasted_iota(I32, (1, t), 1)
    row = lax.broadcasted_iota(I32, (LANE, t), 0)
    for h in range(N_HEADS):
        qz_ref[h // 2, :, (h % 2) * t:(h % 2 + 1) * t] = jnp.where(
            (row // HEAD_DIM) == (h % 2), _pair_rows(q_ref, h), 0).astype(BF16)
        iqz_ref[:, h * t:(h + 1) * t] = jnp.concatenate(
            [iq_ref[h * D_IDX:(h + 1) * D_IDX, :], jnp.zeros((LANE - D_IDX, t), BF16)], axis=0)

    iw = iw_ref[...]

    gm_ref[...] = jnp.full((t, t), INT_MIN, I32)

    def score_block(c, _, masked):
        k0 = pl.multiple_of(c * t, t)
        a = _dot(ik_ref[pl.ds(k0, t), :], iqz_ref[...])
        acc = None
        for h in range(N_HEADS):
            term = jnp.maximum(a[:, h * t:(h + 1) * t], 0.0) * iw[h:h + 1, :]
            acc = term if acc is None else acc + term
        bits = lax.bitcast_convert_type(acc, I32)
        key = jnp.where(bits < 0, bits ^ INT_MAX, bits)
        key = key - (key >> 31)
        if masked:
            key = jnp.where(keep, key, INT_MIN)
        keys_ref[c] = key
        gm_ref[...] = jnp.maximum(gm_ref[...], key)
        return 0

    lax.fori_loop(0, i, functools.partial(score_block, masked=False), 0)
    score_block(i, 0, True)

    g = gm_ref[...]
    while g.shape[0] // 2 >= top_k:
        half = g.shape[0] // 2
        g = jnp.maximum(g[:half], g[half:])
    assert g.shape[0] >= top_k

    def fold_rows(x, op):
        return op(op(x.reshape(x.shape[0] // 8, 8, t), axis=0).astype(F32), axis=0, keepdims=True)

    margin, edge = 256.0, 2.0 ** 31 - 1024.0
    lo_b = jnp.maximum(fold_rows(g, jnp.min) - margin, -edge).astype(I32)
    hi_b = jnp.minimum(fold_rows(g, jnp.max) + margin, edge).astype(I32)

    def count(pred):
        def body(c, acc):
            return acc + jnp.sum(pred(keys_ref[c], c).reshape(t // 8, 8, t), axis=0)
        acc = lax.fori_loop(0, nblk, body, jnp.zeros((8, t), F32))
        return jnp.sum(acc, axis=0, keepdims=True).astype(I32)

    n_valid = q_pos + 1
    small = n_valid <= top_k
    lo0 = jnp.full((1, t), INT_MIN + 1, I32)
    state0 = dict(lo=lo0, hi=hi_b, clo=n_valid, chi=jnp.zeros((1, t), I32),
                  thr=lo0, need=jnp.zeros((1, t), I32), tied=jnp.zeros((1, t), I32), done=small.astype(I32))

    def n_active(done):
        return jnp.sum((1 - done).astype(F32))

    def body(c):
        _, st = c
        lo, hi, clo, chi, done = st["lo"], st["hi"], st["clo"], st["chi"], st["done"]
        mid = _floor_avg(lo, hi)
        mid = jnp.where((lo < 0) & (hi > 0), 0, mid)
        mid = jnp.where((lo == 0) & (hi > 1), 1, mid)
        first = (lo == INT_MIN + 1) & (lo_b > INT_MIN + 1)
        mid = jnp.where(first, jnp.minimum(lo_b, hi - 1), mid)
        cnt = count(lambda kc, _c: jnp.where(kc >= mid, 1.0, 0.0))
        act = done == 0
        ge = cnt >= top_k
        lo_n = jnp.where(ge, mid, lo)
        clo_n = jnp.where(ge, cnt, clo)
        hi_n = jnp.where(ge, hi, mid)
        chi_n = jnp.where(ge, chi, cnt)
        exact = cnt == top_k
        fin = act & (exact | (hi_n == lo_n + 1))
        upd = lambda new, old: jnp.where(act, new, old)
        st_n = dict(lo=upd(lo_n, lo), hi=upd(hi_n, hi), clo=upd(clo_n, clo), chi=upd(chi_n, chi),
                    thr=jnp.where(fin, jnp.where(exact, mid, lo_n), st["thr"]),
                    need=jnp.where(fin, top_k - chi_n, st["need"]),
                    tied=jnp.where(fin, ((~exact) & (clo_n > top_k)).astype(I32), st["tied"]),
                    done=jnp.where(fin, 1, done))
        return n_active(st_n["done"]), st_n

    _, st = lax.while_loop(lambda c: c[0] > 0.5, body, (n_active(state0["done"]), state0))
    thr, need, tied = st["thr"], st["need"], st["tied"] > 0

    @pl.when(jnp.sum(tied.astype(F32)) > 0.5)
    def _():
        tri = jnp.where(lax.broadcasted_iota(I32, (t, t), 1) <= key_row, 1.0, 0.0).astype(BF16)
        need_f = jnp.where(tied, need, 2 * L).astype(F32)

        def fbody(c, before):
            kc = keys_ref[c]
            e = jnp.where(kc == thr, 1.0, 0.0)
            rank = _dot(tri, e.astype(BF16)) + before
            keys_ref[c] = kc - jnp.where(rank > need_f, e, 0.0).astype(I32)
            return before + jnp.sum(e, axis=0, keepdims=True)

        lax.fori_loop(0, nblk, fbody, jnp.zeros((1, t), F32))

    def attn_block(c, carry, near):
        k0 = pl.multiple_of(c * t, t)
        kk = k_ref[pl.ds(k0, t), :]
        vt = v_ref[0, c]
        sel = keys_ref[c] >= thr
        sp = [_dot(kk[:, p * LANE:(p + 1) * LANE], qz_ref[p]) for p in range(N_HEADS // 2)]
        out = []
        for h in range(N_HEADS):
            s = sp[h // 2][:, (h % 2) * t:(h % 2 + 1) * t]
            if near:
                s = s + dtab_ref[h, i - c]
            s = jnp.where(sel, s, NEG)
            out.append(_online_update(carry[h], s, vt[h * HEAD_DIM:(h + 1) * HEAD_DIM, :]))
        return tuple(out)

    n_far = jnp.maximum(i - 1, 0)
    carry = lax.fori_loop(0, n_far, functools.partial(attn_block, near=False),
                          tuple(_softmax_init(t) for _ in range(N_HEADS)))
    carry = lax.fori_loop(n_far, nblk, functools.partial(attn_block, near=True), carry)
    ot = jnp.concatenate([acc / l for (_, l, acc) in carry], axis=0)
    o_ref[...] = ot.T.astype(o_ref.dtype)


def _dsa(qt, ka, vt, gt, dtab, B, L, t, top_k):
    nq = L // t
    return pl.pallas_call(
        functools.partial(_dsa_kernel, t=t, top_k=top_k, L=L),
        grid=(B, nq),
        in_specs=[_q_spec(L, t, QT_GROUP["s"]), _k_spec(L, KA_BLOCK["s"]), _v_spec(L, t, 2),
                  _q_spec(L, t, QT_GROUP["i"]),
                  pl.BlockSpec((L, LANE), lambda b, i: (b, IK_BLOCK)),
                  pl.BlockSpec((GT_ROWS, t), lambda b, i: (0, b * nq + i)),
                  pl.BlockSpec((N_HEADS, 2, t, t), lambda b, i: (0, 0, 0, 0))],
        out_specs=_out_spec(L, t),
        out_shape=jax.ShapeDtypeStruct((B * L, GROUP), BF16),
        scratch_shapes=[pltpu.VMEM((nq, t, t), I32), pltpu.VMEM((N_HEADS // 2, LANE, 2 * t), BF16),
                        pltpu.VMEM((LANE, N_HEADS * t), BF16), pltpu.VMEM((t, t), I32)],
        compiler_params=_cparams(("parallel", "arbitrary")),
        name="dsa",
    )(qt, ka, vt, qt, ka, gt, dtab)


def _conv_kernel(u_ref, halo_ref, w_ref, cb_ref, g_ref, b_ref, o_ref, h_ref, *, tc):
    i = pl.program_id(1)

    def glu(u):
        return u[:, :GROUP] * jax.nn.sigmoid(u[:, GROUP:])

    h_ref[pl.ds(CONV_HALO, tc), :] = glu(u_ref[...])
    h_ref[pl.ds(0, CONV_HALO), :] = jnp.where(i > 0, glu(halo_ref[...]), 0.0)
    acc = jnp.zeros((tc, GROUP), F32)
    for j in range(CONV_W):
        acc = acc + h_ref[pl.ds(CONV_HALO - (CONV_W - 1) + j, tc), :] * w_ref[j:j + 1, :]
    y = _layernorm(acc + cb_ref[...], g_ref[...], b_ref[...])
    o_ref[...] = (y * jax.nn.sigmoid(y)).astype(o_ref.dtype)


def _conv(pb, conv_w, cb, g, b, B, L, tc):
    nc = L // tc
    r = tc // CONV_HALO
    row = lambda a: a.reshape(1, GROUP)
    return pl.pallas_call(
        functools.partial(_conv_kernel, tc=tc),
        grid=(B, nc),
        in_specs=[pl.BlockSpec((tc, 2 * GROUP), lambda bb, i: (bb * nc + i, 0)),
                  pl.BlockSpec((CONV_HALO, 2 * GROUP), lambda bb, i: (jnp.maximum((bb * nc + i) * r - 1, 0), 0)),
                  pl.BlockSpec((CONV_W, GROUP), lambda bb, i: (0, 0)),
                  pl.BlockSpec((1, GROUP), lambda bb, i: (0, 0)),
                  pl.BlockSpec((1, GROUP), lambda bb, i: (0, 0)),
                  pl.BlockSpec((1, GROUP), lambda bb, i: (0, 0))],
        out_specs=pl.BlockSpec((tc, GROUP), lambda bb, i: (bb * nc + i, 0)),
        out_shape=jax.ShapeDtypeStruct((B * L, GROUP), BF16),
        scratch_shapes=[pltpu.VMEM((tc + CONV_HALO, GROUP), F32)],
        compiler_params=_cparams(("parallel", "arbitrary")),
        name="conv",
    )(pb, pb, conv_w, row(cb), row(g), row(b))


def _mix_kernel(yf_ref, yd_ref, ys_ref, yc_ref, x_ref, wo_ref, g_ref, b_ref, rw_ref, rb_ref,
                x1_ref, idx_ref, wt_ref, *, alpha):
    mix = None
    for n, y_ref in enumerate((yf_ref, yd_ref, ys_ref, yc_ref)):
        part = _dot(y_ref[...], wo_ref[n * GROUP:(n + 1) * GROUP, :])
        mix = part if mix is None else mix + part
    x1 = _layernorm(alpha * x_ref[...] + mix, g_ref[...], b_ref[...])
    x1_ref[...] = x1

    a1, a2, a3 = _split3(x1)
    w1, w2, w3 = rw_ref[0], rw_ref[1], rw_ref[2]
    lg = (_dot(a1, w1) + (_dot(a1, w2) + _dot(a2, w1)) + (_dot(a1, w3) + _dot(a2, w2) + _dot(a3, w1))) + rb_ref[...]

    tm = lg.shape[0]
    lane = lax.broadcasted_iota(I32, (tm, LANE), 1)
    lane_f = lane.astype(F32)
    idx_out = jnp.zeros((tm, LANE), I32)
    vals = []
    for r in range(TOP_K):
        mx = jnp.max(lg, axis=1, keepdims=True)
        am = jnp.min(jnp.where(lg == mx, lane_f, float(LANE)), axis=1, keepdims=True).astype(I32)
        vals.append(mx)
        idx_out = jnp.where(lane == r, am, idx_out)
        lg = jnp.where(lane == am, 2 * NEG, lg)
    es = [jnp.exp(v - vals[0]) for v in vals]
    den = es[0] + es[1] + es[2] + es[3]
    wt = jnp.zeros((tm, LANE), F32)
    for r in range(TOP_K):
        wt = jnp.where(lane == r, es[r] / den, wt)
    idx_ref[...] = idx_out
    wt_ref[...] = wt


def _mix(ys, x2d, wo, g, b, rw3, rb, alpha, tm):
    T = x2d.shape[0]
    yspec = pl.BlockSpec((tm, GROUP), lambda i: (i, 0))
    full = lambda shp: pl.BlockSpec(shp, lambda i: (0,) * len(shp))
    return pl.pallas_call(
        functools.partial(_mix_kernel, alpha=alpha),
        grid=(T // tm,),
        in_specs=[yspec, yspec, yspec, yspec,
                  pl.BlockSpec((tm, D_MODEL), lambda i: (i, 0)),
                  full((D_MODEL, D_MODEL)), full((1, D_MODEL)), full((1, D_MODEL)),
                  full((3, D_MODEL, LANE)), full((1, LANE))],
        out_specs=[pl.BlockSpec((tm, D_MODEL), lambda i: (i, 0)),
                   pl.BlockSpec((tm, LANE), lambda i: (i, 0)),
                   pl.BlockSpec((tm, LANE), lambda i: (i, 0))],
        out_shape=[jax.ShapeDtypeStruct((T, D_MODEL), F32),
                   jax.ShapeDtypeStruct((T, LANE), I32),
                   jax.ShapeDtypeStruct((T, LANE), F32)],
        compiler_params=_cparams(("parallel",)),
        name="mix",
    )(*ys, x2d, wo, g.reshape(1, D_MODEL), b.reshape(1, D_MODEL), rw3, rb)


def _start_row_gather(idx_ref, base, n, src_hbm, dst_ref, sem, inline=False):
    def body(r, _):
        pltpu.make_async_copy(src_hbm.at[pl.ds(idx_ref[base + r], 1)], dst_ref.at[pl.ds(r, 1)], sem).start()
        return 0
    if inline:
        for r in range(n):
            body(r, 0)
    else:
        lax.fori_loop(0, n, body, 0, unroll=8)


def _wait_row_gather(n, src_hbm, dst_ref, sem):
    pltpu.make_async_copy(src_hbm.at[pl.ds(0, n)], dst_ref, sem).wait()


def _expert_kernel(te_ref, tv_ref, tb_ref, tok_ref, x_hbm, wt_ref, bg_ref, bu_ref, wd_ref, bd_ref, y_ref,
                   xbuf, sem, wg_s, wu_s, wd_s, *, tmm):
    i = pl.program_id(0)
    n = pl.num_programs(0)
    slot = lax.rem(i, 2)

    @pl.when(i == 0)
    def _():
        _start_row_gather(tok_ref, tb_ref[0], tmm, x_hbm, xbuf.at[0], sem.at[0])

    _wait_row_gather(tmm, x_hbm, xbuf.at[slot], sem.at[slot])
    prefetch = functools.partial(_start_row_gather, tok_ref, tb_ref[i + 1], tmm, x_hbm, xbuf.at[1 - slot],
                                 sem.at[1 - slot])

    @pl.when((tv_ref[i] > 0) & ((i == 0) | (te_ref[i] != te_ref[jnp.maximum(i - 1, 0)])))
    def _():
        for c in range(D_MODEL // LANE):
            rows = slice(c * LANE, (c + 1) * LANE)
            wg_s[rows, :] = wt_ref[0, c, pl.ds(0, D_MODEL, stride=2), :].T.astype(BF16)
            wu_s[rows, :] = wt_ref[0, c, pl.ds(1, D_MODEL, stride=2), :].T.astype(BF16)
        wd_s[...] = wd_ref[0].astype(BF16)

    @pl.when(tv_ref[i] > 0)
    def _():
        prefetch(inline=True)
        xb = xbuf[slot].astype(BF16)
        gate = jnp.minimum(_dot(xb, wg_s[...]) + bg_ref[0], SWIGLU_LIMIT)
        up = jnp.clip(_dot(xb, wu_s[...]) + bu_ref[0], -SWIGLU_LIMIT, SWIGLU_LIMIT)
        h = (up + 1.0) * gate * jax.nn.sigmoid(gate * SWIGLU_ALPHA)
        y_ref[...] = _dot(h.astype(BF16), wd_s[...]) + bd_ref[0]

    @pl.when(tv_ref[i] == 0)
    def _():
        prefetch()
        y_ref[...] = jnp.zeros_like(y_ref)

    @pl.when(i == n - 1)
    def _():
        _wait_row_gather(tmm, x_hbm, xbuf.at[1 - slot], sem.at[1 - slot])


def _experts(tile_e, tile_v, tile_base, tok, x1, wgu_t, bg, bu, wd, bd, R, tmm):
    by_expert = lambda shp: pl.BlockSpec((1,) + shp, lambda i, te, tv, tb, tk: (te[i],) + (0,) * len(shp))
    wscratch = pltpu.VMEM((D_MODEL, D_MODEL), BF16)
    return pl.pallas_call(
        functools.partial(_expert_kernel, tmm=tmm),
        grid_spec=pltpu.PrefetchScalarGridSpec(
            num_scalar_prefetch=4,
            grid=(R // tmm,),
            in_specs=[pl.BlockSpec(memory_space=pl.ANY),
                      by_expert((D_MODEL // LANE, 2 * D_MODEL, LANE)), by_expert((1, D_MODEL)),
                      by_expert((1, D_MODEL)),
                      by_expert((D_MODEL, D_MODEL)), by_expert((1, D_MODEL))],
            out_specs=pl.BlockSpec((tmm, D_MODEL), lambda i, te, tv, tb, tk: (i, 0)),
            scratch_shapes=[pltpu.VMEM((2, tmm, D_MODEL), F32), pltpu.SemaphoreType.DMA((2,)),
                            wscratch, wscratch, wscratch],
        ),
        out_shape=jax.ShapeDtypeStruct((R, D_MODEL), F32),
        compiler_params=_cparams(("arbitrary",)),
        name="moe_experts",
    )(tile_e, tile_v, tile_base, tok, x1, wgu_t, bg, bu, wd, bd)


def _combine_kernel(pos_ref, ys_hbm, x1_ref, wt_ref, g_ref, b_ref, o_ref, buf, sem, *, tc, alpha):
    i = pl.program_id(0)
    n = pl.num_programs(0)
    slot = lax.rem(i, 2)
    rows = tc * TOP_K

    @pl.when(i == 0)
    def _():
        _start_row_gather(pos_ref, 0, rows, ys_hbm, buf.at[0], sem.at[0])

    @pl.when(i + 1 < n)
    def _():
        _start_row_gather(pos_ref, (i + 1) * rows, rows, ys_hbm, buf.at[1 - slot], sem.at[1 - slot], inline=True)

    _wait_row_gather(rows, ys_hbm, buf.at[slot], sem.at[slot])
    wt = wt_ref[...]
    ffn = buf[slot, pl.ds(0, tc), :] * wt[:, 0:1]
    for k in range(1, TOP_K):
        ffn = ffn + buf[slot, pl.ds(k * tc, tc), :] * wt[:, k:k + 1]
    o_ref[...] = _layernorm(alpha * x1_ref[...] + ffn, g_ref[...], b_ref[...])


def _combine(pos, ys, x1, wt, g, b, alpha, tc):
    T = x1.shape[0]
    pos_t = jnp.transpose(pos.reshape(T // tc, tc, TOP_K), (0, 2, 1)).reshape(T * TOP_K)
    return pl.pallas_call(
        functools.partial(_combine_kernel, tc=tc, alpha=alpha),
        grid_spec=pltpu.PrefetchScalarGridSpec(
            num_scalar_prefetch=1,
            grid=(T // tc,),
            in_specs=[pl.BlockSpec(memory_space=pl.ANY),
                      pl.BlockSpec((tc, D_MODEL), lambda i, p: (i, 0)),
                      pl.BlockSpec((tc, LANE), lambda i, p: (i, 0)),
                      pl.BlockSpec((1, D_MODEL), lambda i, p: (0, 0)),
                      pl.BlockSpec((1, D_MODEL), lambda i, p: (0, 0))],
            out_specs=pl.BlockSpec((tc, D_MODEL), lambda i, p: (i, 0)),
            scratch_shapes=[pltpu.VMEM((2, TOP_K * tc, D_MODEL), F32), pltpu.SemaphoreType.DMA((2,))],
        ),
        out_shape=jax.ShapeDtypeStruct((T, D_MODEL), F32),
        compiler_params=_cparams(("arbitrary",)),
        name="moe_combine",
    )(pos_t, ys, x1, wt, g.reshape(1, D_MODEL), b.reshape(1, D_MODEL))


def _moe_plan(top_idx, tmm):
    T = top_idx.shape[0]
    F = T * TOP_K
    R = F + N_EXPERTS * tmm
    flat_e = top_idx.reshape(F)
    onehot = (flat_e[:, None] == jnp.arange(N_EXPERTS, dtype=I32)[None, :]).astype(I32)
    csum = jnp.cumsum(onehot, axis=0)
    counts = csum[-1]
    padded = ((counts + tmm - 1) // tmm) * tmm
    gend = jnp.cumsum(padded)
    gstart = gend - padded
    ustart = jnp.cumsum(counts) - counts
    pos = jnp.sum(onehot * (csum - 1 + gstart[None, :]), axis=1)
    order = jnp.argsort(flat_e, stable=True).astype(I32)
    n_tiles = R // tmm + 1
    t_start = jnp.arange(n_tiles, dtype=I32) * tmm
    t_e = jnp.minimum(jnp.sum((t_start[:, None] >= gend[None, :]).astype(I32), axis=1), N_EXPERTS - 1)
    tile_base = jnp.clip(ustart[t_e] + t_start - gstart[t_e], 0, F)
    tok = jnp.concatenate([order // TOP_K, jnp.zeros((tmm,), I32)])
    tile_v = (t_start[:-1] < gend[-1]).astype(I32)
    last_e = jnp.sum((gend[-1] - 1 >= gend).astype(I32))
    tile_e = jnp.where(tile_v > 0, t_e[:-1], last_e)
    return pos.astype(I32), tok, tile_base.astype(I32), tile_e.astype(I32), tile_v, R


def _rel_bucket(dist):
    n = jnp.maximum(dist, 0)
    max_exact = NUM_BUCKETS // 2
    nf = jnp.maximum(n, 1).astype(F32)
    large = max_exact + (jnp.log(nf / max_exact) / math.log(MAX_DISTANCE / max_exact)
                         * (NUM_BUCKETS - max_exact)).astype(I32)
    large = jnp.minimum(large, NUM_BUCKETS - 1)
    return jnp.where(n < max_exact, n, large)


def _bias_tables(table, t):
    assert t >= MAX_DISTANCE, "key blocks two or more away must lie entirely in the last bucket"
    d0 = jnp.arange(t, dtype=I32)[None, :] - jnp.arange(t, dtype=I32)[:, None]
    rel = (table - table[NUM_BUCKETS - 1][None, :]) * LOG2E
    out = []
    for d in (d0, d0 + t):
        bucket = _rel_bucket(d)[None]
        b = jnp.zeros((table.shape[1], t, t), F32)
        for n in range(NUM_BUCKETS):
            b = jnp.where(bucket == n, rel[n][:, None, None], b)
        out.append(jnp.where((d >= 0)[None], b, NEG))
    return jnp.stack(out, axis=1).astype(F32)


def kernel(x, rel_bias, w_in, forget_b, diff_lambda, diff_norm_g, conv_w, conv_b, conv_ln_g, conv_ln_b,
           w_out, ln1_g, ln1_b, router_w, router_b, w_gu, b_gu, w_down, b_down, ln2_g, ln2_b):
    B, L, D = x.shape
    depth = w_in.shape[0]
    assert D == D_MODEL
    T = B * L
    t = 512 if L % 512 == 0 else 256
    tm = 256
    tmm = 256
    tcv = 512 if L % 512 == 0 else 256
    assert L % t == 0 and T % tm == 0
    top_k = min(K_SEL_MAX, L // 4)
    alpha = (2 * depth) ** 0.25

    dtab_diff = _bias_tables(rel_bias[:, :N_HEADS], t)
    dtab_dsa = _bias_tables(rel_bias[:, N_HEADS:], t)

    x2d = x.reshape(T, D)
    for l in range(depth):
        lambda_init = 0.8 - 0.6 * math.exp(-0.3 * l)
        qt, ka, vt, pb, gt = _proj(x2d, _proj_weights(w_in[l]), t)

        fb_row = jnp.zeros((1, LANE), F32).at[0, :N_HEADS].set(forget_b[l])
        caug = _foxgate(pb, fb_row, B, L)
        y_fox = _fox(qt, ka, caug, vt, B, L, t)

        lp = diff_lambda[l].astype(F32)
        lam = (jnp.exp(jnp.sum(lp[0] * lp[1])) - jnp.exp(jnp.sum(lp[2] * lp[3])) + lambda_init).reshape(1)
        g_tile = jnp.broadcast_to(diff_norm_g[l].astype(F32)[:, None], (HEAD_DIM, t))
        y_diff = _diff(lam, qt, ka, vt, dtab_diff, g_tile, B, L, t, 1.0 - lambda_init)

        y_dsa = _dsa(qt, ka, vt, gt, dtab_dsa, B, L, t, top_k)
        y_conv = _conv(pb, conv_w[l], conv_b[l], conv_ln_g[l], conv_ln_b[l], B, L, tcv)

        rw = jnp.zeros((D, LANE), F32).at[:, :N_EXPERTS].set(router_w[l])
        r1 = rw.astype(BF16)
        r2 = (rw - r1.astype(F32)).astype(BF16)
        r3 = (rw - r1.astype(F32) - r2.astype(F32)).astype(BF16)
        rb = jnp.full((1, LANE), NEG, F32).at[0, :N_EXPERTS].set(router_b[l])
        x1, top_idx, top_w = _mix((y_fox, y_diff, y_dsa, y_conv), x2d, w_out[l].astype(BF16),
                                  ln1_g[l], ln1_b[l], jnp.stack([r1, r2, r3]), rb, alpha, tm)

        pos, tok, tile_base, tile_e, tile_v, R = _moe_plan(top_idx[:, :TOP_K], tmm)
        wgu_t = jnp.transpose(w_gu[l].reshape(N_EXPERTS, D // LANE, LANE, 2 * D), (0, 1, 3, 2))
        ys = _experts(tile_e, tile_v, tile_base, tok, x1, wgu_t, b_gu[l][:, None, 0::2],
                      b_gu[l][:, None, 1::2], w_down[l], b_down[l][:, None, :], R, tmm)
        x2d = _combine(pos, ys, x1, top_w, ln2_g[l], ln2_b[l], alpha, 128)
    return x2d.reshape(B, L, D)
```

```python
import functools
import math

import numpy as np
import jax
import jax.numpy as jnp
from jax import lax
from jax.experimental import pallas as pl
from jax.experimental.pallas import tpu as pltpu

F32 = jnp.float32
BF16 = jnp.bfloat16
I32 = jnp.int32

D_MODEL = 1024
HEAD_DIM = 64
N_HEADS = 4
GROUP = N_HEADS * HEAD_DIM
DIFF_QK = HEAD_DIM // 2
D_IDX = 64
K_SEL_MAX = 256
CONV_W = 31
CONV_HALO = 32
NUM_BUCKETS = 32
MAX_DISTANCE = 128
N_EXPERTS = 32
TOP_K = 4
SWIGLU_LIMIT = 7.0
SWIGLU_ALPHA = 1.702
LN_EPS = 1e-5
LOG2E = math.log2(math.e)
NEG = -1e30
INT_MIN = -(2 ** 31)
INT_MAX = 2 ** 31 - 1
LANE = 128
VMEM_LIMIT = 56 * 1024 * 1024

_IN_WIDTHS = (GROUP, GROUP, GROUP, N_HEADS, GROUP, GROUP, GROUP, GROUP, GROUP, GROUP,
              N_HEADS * D_IDX, D_IDX, N_HEADS, 2 * GROUP)
_IN_OFF = tuple(int(v) for v in np.cumsum((0,) + _IN_WIDTHS))
(_FQ, _FK, _FV, _FF, _DQ, _DK, _DV, _SQ, _SK, _SV, _IQ, _IK, _IW, _CU) = range(14)

QT_GROUP = {"f": 0, "d": 1, "s": 2, "i": 3}
KA_BLOCK = {"f": 0, "d": 1, "s": 2}
NK = 3 * GROUP + LANE
IK_BLOCK = 3 * GROUP // LANE
NB = 2 * GROUP + LANE
GATE_BLOCK = 2 * GROUP // LANE
GT_ROWS = 16


def _cols(w, slots, scales=None):
    parts = []
    for n, s in enumerate(slots):
        part = w[:, _IN_OFF[s]:_IN_OFF[s + 1]]
        parts.append(part if scales is None or scales[n] == 1.0 else part * scales[n])
    return jnp.concatenate(parts, axis=1)


def _proj_weights(w):
    qs = HEAD_DIM ** -0.5 * LOG2E
    iw_scale = N_HEADS ** -0.5
    wq_t = _cols(w, (_FQ, _DQ, _SQ, _IQ), (qs, DIFF_QK ** -0.5 * LOG2E, qs, D_IDX ** -0.5)).T.astype(BF16)
    wk = jnp.pad(_cols(w, (_FK, _DK, _SK, _IK)), ((0, 0), (0, LANE - D_IDX))).astype(BF16)
    wv_t = _cols(w, (_FV, _DV, _SV)).T.astype(BF16)
    wb = jnp.pad(_cols(w, (_CU, _FF, _IW), (1.0, 1.0, iw_scale)),
                 ((0, 0), (0, LANE - 2 * N_HEADS))).astype(BF16)
    wg_t = jnp.pad((w[:, _IN_OFF[_IW]:_IN_OFF[_IW + 1]] * iw_scale).T,
                   ((0, GT_ROWS - N_HEADS), (0, 0))).astype(BF16)
    return wq_t, wk, wv_t, wb, wg_t


def _cparams(sem):
    return pltpu.CompilerParams(dimension_semantics=sem, vmem_limit_bytes=VMEM_LIMIT)


def _dot(a, b):
    return jnp.dot(a, b, preferred_element_type=F32)


def _dot_nt(a, b):
    return lax.dot_general(a, b, (((1,), (1,)), ((), ())), preferred_element_type=F32)


def _split3(a):
    a1 = a.astype(BF16)
    r1 = a - a1.astype(F32)
    a2 = r1.astype(BF16)
    a3 = (r1 - a2.astype(F32)).astype(BF16)
    return a1, a2, a3


def _layernorm(z, g, b):
    mu = jnp.mean(z, axis=-1, keepdims=True)
    zc = z - mu
    var = jnp.mean(zc * zc, axis=-1, keepdims=True)
    return zc * lax.rsqrt(var + LN_EPS) * g + b


def _proj_kernel(x_ref, wq_ref, wk_ref, wv_ref, wb_ref, wg_ref, qt_ref, ka_ref, vt_ref, pb_ref, gt_ref):
    xb = x_ref[...].astype(BF16)
    qt_ref[...] = _dot_nt(wq_ref[...], xb).astype(qt_ref.dtype)
    ka_ref[...] = _dot(xb, wk_ref[...]).astype(ka_ref.dtype)
    vt = _dot_nt(wv_ref[...], xb)
    for g in range(3):
        vt_ref[g, 0] = vt[g * GROUP:(g + 1) * GROUP, :].astype(vt_ref.dtype)
    pb_ref[...] = _dot(xb, wb_ref[...])
    gt_ref[...] = _dot_nt(wg_ref[...], xb)


def _proj(x2d, weights, t):
    T = x2d.shape[0]
    full = lambda a: pl.BlockSpec(a.shape, lambda i: (0,) * a.ndim)
    return pl.pallas_call(
        _proj_kernel,
        grid=(T // t,),
        in_specs=[pl.BlockSpec((t, D_MODEL), lambda i: (i, 0))] + [full(a) for a in weights],
        out_specs=[pl.BlockSpec((4 * GROUP, t), lambda i: (0, i)),
                   pl.BlockSpec((t, NK), lambda i: (i, 0)),
                   pl.BlockSpec((3, 1, GROUP, t), lambda i: (0, i, 0, 0)),
                   pl.BlockSpec((t, NB), lambda i: (i, 0)),
                   pl.BlockSpec((GT_ROWS, t), lambda i: (0, i))],
        out_shape=[jax.ShapeDtypeStruct((4 * GROUP, T), BF16),
                   jax.ShapeDtypeStruct((T, NK), BF16),
                   jax.ShapeDtypeStruct((3, T // t, GROUP, t), BF16),
                   jax.ShapeDtypeStruct((T, NB), F32),
                   jax.ShapeDtypeStruct((GT_ROWS, T), F32)],
        compiler_params=_cparams(("parallel",)),
        name="proj",
    )(x2d, *weights)


def _foxgate_kernel(g_ref, fb_ref, c_ref, *, chunk):
    L = g_ref.shape[0]
    row = lax.broadcasted_iota(I32, (chunk, chunk), 0)
    col = lax.broadcasted_iota(I32, (chunk, chunk), 1)
    tri = jnp.where(col <= row, 1.0, 0.0).astype(BF16)
    lane = lax.broadcasted_iota(I32, (chunk, LANE), 1)

    def body(i, carry):
        r0 = pl.multiple_of(i * chunk, chunk)
        z = g_ref[pl.ds(r0, chunk), :] + fb_ref[...]
        lf = jnp.minimum(z, 0.0) - jnp.log1p(jnp.exp(-jnp.abs(z)))
        p1, p2, p3 = _split3(lf)
        cs = _dot(tri, p1) + _dot(tri, p2) + _dot(tri, p3) + carry
        cl = jnp.where(lane < N_HEADS, cs * LOG2E, 0.0)
        c1 = cl.astype(BF16).astype(F32)
        r1 = cl - c1
        c2 = r1.astype(BF16).astype(F32)
        c3 = r1 - c2
        out = c1 + pltpu.roll(c2, N_HEADS, axis=1) + pltpu.roll(c3, 2 * N_HEADS, axis=1)
        c_ref[pl.ds(r0, chunk), :] = out.astype(c_ref.dtype)
        return cs[chunk - 1:chunk, :]

    lax.fori_loop(0, L // chunk, body, jnp.zeros((1, LANE), F32))


def _foxgate(pb, fb_row, B, L):
    return pl.pallas_call(
        functools.partial(_foxgate_kernel, chunk=256),
        grid=(B,),
        in_specs=[pl.BlockSpec((L, LANE), lambda b: (b, GATE_BLOCK)),
                  pl.BlockSpec((1, LANE), lambda b: (0, 0))],
        out_specs=pl.BlockSpec((L, LANE), lambda b: (b, 0)),
        out_shape=jax.ShapeDtypeStruct((B * L, LANE), BF16),
        compiler_params=_cparams(("parallel",)),
        name="foxgate",
    )(pb, fb_row)


def _online_update(carry, s, vt):
    m, l, acc = carry
    m_new = jnp.maximum(m, jnp.max(s, axis=0, keepdims=True))
    alpha = jnp.exp2(m - m_new)
    p = jnp.exp2(s - m_new)
    l = alpha * l + jnp.sum(p, axis=0, keepdims=True)
    acc = alpha * acc + _dot(vt, p.astype(BF16))
    return m_new, l, acc


def _softmax_init(t):
    return (jnp.full((1, t), NEG, F32), jnp.zeros((1, t), F32), jnp.zeros((HEAD_DIM, t), F32))


def _causal(t):
    return lax.broadcasted_iota(I32, (t, t), 0) <= lax.broadcasted_iota(I32, (t, t), 1)


def _pair_rows(q_ref, h):
    p = h // 2
    return q_ref[p * LANE:(p + 1) * LANE, :]


def _q_spec(L, t, group):
    nq = L // t
    return pl.BlockSpec((GROUP, t), lambda b, i: (group, b * nq + i))


def _k_spec(L, block):
    return pl.BlockSpec((L, GROUP), lambda b, i: (b, block))


def _v_spec(L, t, group):
    return pl.BlockSpec((1, L // t, GROUP, t), lambda b, i: (group, b, 0, 0))


def _out_spec(L, t):
    nq = L // t
    return pl.BlockSpec((t, GROUP), lambda b, i: (b * nq + i, 0))


def _fox_kernel(q_ref, k_ref, ca_ref, v_ref, o_ref, qa_ref, *, t):
    i = pl.program_id(1)
    row = lax.broadcasted_iota(I32, (LANE, t), 0)
    for h in range(N_HEADS):
        qz = jnp.where((row // HEAD_DIM) == (h % 2), _pair_rows(q_ref, h), 0)
        sel = jnp.where((row == h) | (row == N_HEADS + h) | (row == 2 * N_HEADS + h), -1.0, 0.0).astype(BF16)
        qa_ref[h // 2, :, (h % 2) * t:(h % 2 + 1) * t] = jnp.concatenate([qz.astype(BF16), sel], axis=0)
    keep = _causal(t)

    def step(j, carry, masked):
        k0 = pl.multiple_of(j * t, t)
        kk = k_ref[pl.ds(k0, t), :]
        ca = ca_ref[pl.ds(k0, t), :]
        vt = v_ref[0, j]
        sp = [_dot(jnp.concatenate([kk[:, p * LANE:(p + 1) * LANE], ca], axis=1), qa_ref[p])
              for p in range(N_HEADS // 2)]
        out = []
        for h in range(N_HEADS):
            s = sp[h // 2][:, (h % 2) * t:(h % 2 + 1) * t]
            if masked:
                s = jnp.where(keep, s, NEG)
            out.append(_online_update(carry[h], s, vt[h * HEAD_DIM:(h + 1) * HEAD_DIM, :]))
        return tuple(out)

    carry = lax.fori_loop(0, i, functools.partial(step, masked=False),
                          tuple(_softmax_init(t) for _ in range(N_HEADS)))
    carry = step(i, carry, True)
    ot = jnp.concatenate([acc / l for (_, l, acc) in carry], axis=0)
    o_ref[...] = ot.T.astype(o_ref.dtype)


def _fox(qt, ka, caug, vt, B, L, t):
    return pl.pallas_call(
        functools.partial(_fox_kernel, t=t),
        grid=(B, L // t),
        in_specs=[_q_spec(L, t, QT_GROUP["f"]), _k_spec(L, KA_BLOCK["f"]),
                  pl.BlockSpec((L, LANE), lambda b, i: (b, 0)), _v_spec(L, t, 0)],
        out_specs=_out_spec(L, t),
        out_shape=jax.ShapeDtypeStruct((B * L, GROUP), BF16),
        scratch_shapes=[pltpu.VMEM((N_HEADS // 2, 2 * LANE, 2 * t), BF16)],
        compiler_params=_cparams(("parallel", "arbitrary")),
        name="fox",
    )(qt, ka, caug, vt)


def _diff_kernel(lam_ref, q_ref, k_ref, v_ref, dtab_ref, g_ref, o_ref, qz_ref, *, t, out_scale):
    i = pl.program_id(1)
    lam = lam_ref[0]
    row = lax.broadcasted_iota(I32, (LANE, t), 0)
    for h in range(N_HEADS):
        for m in range(2):
            lo = (h % 2) * HEAD_DIM + m * DIFF_QK
            n = 2 * (h % 2) + m
            qz_ref[h // 2, :, n * t:(n + 1) * t] = jnp.where((row >= lo) & (row < lo + DIFF_QK),
                                                            _pair_rows(q_ref, h), 0).astype(BF16)

    pieces = []
    for p in range(N_HEADS // 2):
        def step(j, carry, near, p=p):
            k0 = pl.multiple_of(j * t, t)
            vt = v_ref[0, j]
            sp = _dot(k_ref[pl.ds(k0, t), p * LANE:(p + 1) * LANE], qz_ref[p])
            out = []
            for hh in range(2):
                h = 2 * p + hh
                for m in range(2):
                    n = 2 * hh + m
                    s = sp[:, n * t:(n + 1) * t]
                    if near:
                        s = s + dtab_ref[h, i - j]
                    out.append(_online_update(carry[2 * hh + m], s, vt[h * HEAD_DIM:(h + 1) * HEAD_DIM, :]))
            return tuple(out)

        n_far = jnp.maximum(i - 1, 0)
        carry = lax.fori_loop(0, n_far, functools.partial(step, near=False),
                              tuple(_softmax_init(t) for _ in range(4)))
        carry = lax.fori_loop(n_far, i + 1, functools.partial(step, near=True), carry)
        for hh in range(2):
            (_, l1, a1), (_, l2, a2) = carry[2 * hh], carry[2 * hh + 1]
            o = a1 / l1 - lam * (a2 / l2)
            o = o * lax.rsqrt(jnp.mean(o * o, axis=0, keepdims=True) + LN_EPS)
            pieces.append(o * g_ref[...] * out_scale)
    o_ref[...] = jnp.concatenate(pieces, axis=0).T.astype(o_ref.dtype)


def _diff(lam, qt, ka, vt, dtab, g_tile, B, L, t, out_scale):
    return pl.pallas_call(
        functools.partial(_diff_kernel, t=t, out_scale=out_scale),
        grid=(B, L // t),
        in_specs=[pl.BlockSpec(memory_space=pltpu.SMEM),
                  _q_spec(L, t, QT_GROUP["d"]), _k_spec(L, KA_BLOCK["d"]), _v_spec(L, t, 1),
                  pl.BlockSpec((N_HEADS, 2, t, t), lambda b, i: (0, 0, 0, 0)),
                  pl.BlockSpec((HEAD_DIM, t), lambda b, i: (0, 0))],
        out_specs=_out_spec(L, t),
        out_shape=jax.ShapeDtypeStruct((B * L, GROUP), BF16),
        scratch_shapes=[pltpu.VMEM((N_HEADS // 2, LANE, 4 * t), BF16)],
        compiler_params=_cparams(("parallel", "arbitrary")),
        name="diff",
    )(lam, qt, ka, vt, dtab, g_tile)


def _floor_avg(a, b):
    return (a >> 1) + (b >> 1) + (a & b & 1)


def _dsa_kernel(q_ref, k_ref, v_ref, iq_ref, ik_ref, iw_ref, dtab_ref, o_ref, keys_ref, qz_ref, iqz_ref, gm_ref,
                *, t, top_k, L):
    i = pl.program_id(1)
    nblk = i + 1
    keep = _causal(t)
    key_row = lax.broadcasted_iota(I32, (t, t), 0)
    q_pos = i * t + lax.broadcasted_iota(I32, (1, t), 1)
    row = lax.broadcasted_iota(I32, (LANE, t), 0)
    for h in range(N_HEADS):
        qz_ref[h // 2, :, (h % 2) * t:(h % 2 + 1) * t] = jnp.where(
            (row // HEAD_DIM) == (h % 2), _pair_rows(q_ref, h), 0).astype(BF16)
        iqz_ref[:, h * t:(h + 1) * t] = jnp.concatenate(
            [iq_ref[h * D_IDX:(h + 1) * D_IDX, :], jnp.zeros((LANE - D_IDX, t), BF16)], axis=0)

    iw = iw_ref[...]

    gm_ref[...] = jnp.full((t, t), INT_MIN, I32)

    def score_block(c, _, masked):
        k0 = pl.multiple_of(c * t, t)
        a = _dot(ik_ref[pl.ds(k0, t), :], iqz_ref[...])
        acc = None
        for h in range(N_HEADS):
            term = jnp.maximum(a[:, h * t:(h + 1) * t], 0.0) * iw[h:h + 1, :]
            acc = term if acc is None else acc + term
        bits = lax.bitcast_convert_type(acc, I32)
        key = jnp.where(bits < 0, bits ^ INT_MAX, bits)
        key = key - (key >> 31)
        if masked:
            key = jnp.where(keep, key, INT_MIN)
        keys_ref[c] = key
        gm_ref[...] = jnp.maximum(gm_ref[...], key)
        return 0

    lax.fori_loop(0, i, functools.partial(score_block, masked=False), 0)
    score_block(i, 0, True)

    g = gm_ref[...]
    while g.shape[0] // 2 >= top_k:
        half = g.shape[0] // 2
        g = jnp.maximum(g[:half], g[half:])
    assert g.shape[0] >= top_k

    def fold_rows(x, op):
        return op(op(x.reshape(x.shape[0] // 8, 8, t), axis=0).astype(F32), axis=0, keepdims=True)

    margin, edge = 256.0, 2.0 ** 31 - 1024.0
    lo_b = jnp.maximum(fold_rows(g, jnp.min) - margin, -edge).astype(I32)
    hi_b = jnp.minimum(fold_rows(g, jnp.max) + margin, edge).astype(I32)

    def count(pred):
        def body(c, acc):
            return acc + jnp.sum(pred(keys_ref[c], c).reshape(t // 8, 8, t), axis=0)
        acc = lax.fori_loop(0, nblk, body, jnp.zeros((8, t), F32))
        return jnp.sum(acc, axis=0, keepdims=True).astype(I32)

    n_valid = q_pos + 1
    small = n_valid <= top_k
    lo0 = jnp.full((1, t), INT_MIN + 1, I32)
    state0 = dict(lo=lo0, hi=hi_b, clo=n_valid, chi=jnp.zeros((1, t), I32),
                  thr=lo0, need=jnp.zeros((1, t), I32), tied=jnp.zeros((1, t), I32), done=small.astype(I32))

    def n_active(done):
        return jnp.sum((1 - done).astype(F32))

    def body(c):
        _, st = c
        lo, hi, clo, chi, done = st["lo"], st["hi"], st["clo"], st["chi"], st["done"]
        mid = _floor_avg(lo, hi)
        mid = jnp.where((lo < 0) & (hi > 0), 0, mid)
        mid = jnp.where((lo == 0) & (hi > 1), 1, mid)
        first = (lo == INT_MIN + 1) & (lo_b > INT_MIN + 1)
        mid = jnp.where(first, jnp.minimum(lo_b, hi - 1), mid)
        cnt = count(lambda kc, _c: jnp.where(kc >= mid, 1.0, 0.0))
        act = done == 0
        ge = cnt >= top_k
        lo_n = jnp.where(ge, mid, lo)
        clo_n = jnp.where(ge, cnt, clo)
        hi_n = jnp.where(ge, hi, mid)
        chi_n = jnp.where(ge, chi, cnt)
        exact = cnt == top_k
        fin = act & (exact | (hi_n == lo_n + 1))
        upd = lambda new, old: jnp.where(act, new, old)
        st_n = dict(lo=upd(lo_n, lo), hi=upd(hi_n, hi), clo=upd(clo_n, clo), chi=upd(chi_n, chi),
                    thr=jnp.where(fin, jnp.where(exact, mid, lo_n), st["thr"]),
                    need=jnp.where(fin, top_k - chi_n, st["need"]),
                    tied=jnp.where(fin, ((~exact) & (clo_n > top_k)).astype(I32), st["tied"]),
                    done=jnp.where(fin, 1, done))
        return n_active(st_n["done"]), st_n

    _, st = lax.while_loop(lambda c: c[0] > 0.5, body, (n_active(state0["done"]), state0))
    thr, need, tied = st["thr"], st["need"], st["tied"] > 0

    @pl.when(jnp.sum(tied.astype(F32)) > 0.5)
    def _():
        tri = jnp.where(lax.broadcasted_iota(I32, (t, t), 1) <= key_row, 1.0, 0.0).astype(BF16)
        need_f = jnp.where(tied, need, 2 * L).astype(F32)

        def fbody(c, before):
            kc = keys_ref[c]
            e = jnp.where(kc == thr, 1.0, 0.0)
            rank = _dot(tri, e.astype(BF16)) + before
            keys_ref[c] = kc - jnp.where(rank > need_f, e, 0.0).astype(I32)
            return before + jnp.sum(e, axis=0, keepdims=True)

        lax.fori_loop(0, nblk, fbody, jnp.zeros((1, t), F32))

    def attn_block(c, carry, near):
        k0 = pl.multiple_of(c * t, t)
        kk = k_ref[pl.ds(k0, t), :]
        vt = v_ref[0, c]
        sel = keys_ref[c] >= thr
        sp = [_dot(kk[:, p * LANE:(p + 1) * LANE], qz_ref[p]) for p in range(N_HEADS // 2)]
        out = []
        for h in range(N_HEADS):
            s = sp[h // 2][:, (h % 2) * t:(h % 2 + 1) * t]
            if near:
                s = s + dtab_ref[h, i - c]
            s = jnp.where(sel, s, NEG)
            out.append(_online_update(carry[h], s, vt[h * HEAD_DIM:(h + 1) * HEAD_DIM, :]))
        return tuple(out)

    n_far = jnp.maximum(i - 1, 0)
    carry = lax.fori_loop(0, n_far, functools.partial(attn_block, near=False),
                          tuple(_softmax_init(t) for _ in range(N_HEADS)))
    carry = lax.fori_loop(n_far, nblk, functools.partial(attn_block, near=True), carry)
    ot = jnp.concatenate([acc / l for (_, l, acc) in carry], axis=0)
    o_ref[...] = ot.T.astype(o_ref.dtype)


def _dsa(qt, ka, vt, gt, dtab, B, L, t, top_k):
    nq = L // t
    return pl.pallas_call(
        functools.partial(_dsa_kernel, t=t, top_k=top_k, L=L),
        grid=(B, nq),
        in_specs=[_q_spec(L, t, QT_GROUP["s"]), _k_spec(L, KA_BLOCK["s"]), _v_spec(L, t, 2),
                  _q_spec(L, t, QT_GROUP["i"]),
                  pl.BlockSpec((L, LANE), lambda b, i: (b, IK_BLOCK)),
                  pl.BlockSpec((GT_ROWS, t), lambda b, i: (0, b * nq + i)),
                  pl.BlockSpec((N_HEADS, 2, t, t), lambda b, i: (0, 0, 0, 0))],
        out_specs=_out_spec(L, t),
        out_shape=jax.ShapeDtypeStruct((B * L, GROUP), BF16),
        scratch_shapes=[pltpu.VMEM((nq, t, t), I32), pltpu.VMEM((N_HEADS // 2, LANE, 2 * t), BF16),
                        pltpu.VMEM((LANE, N_HEADS * t), BF16), pltpu.VMEM((t, t), I32)],
        compiler_params=_cparams(("parallel", "arbitrary")),
        name="dsa",
    )(qt, ka, vt, qt, ka, gt, dtab)


def _conv_kernel(u_ref, halo_ref, w_ref, cb_ref, g_ref, b_ref, o_ref, h_ref, *, tc):
    i = pl.program_id(1)

    def glu(u):
        return u[:, :GROUP] * jax.nn.sigmoid(u[:, GROUP:])

    h_ref[pl.ds(CONV_HALO, tc), :] = glu(u_ref[...])
    h_ref[pl.ds(0, CONV_HALO), :] = jnp.where(i > 0, glu(halo_ref[...]), 0.0)
    acc = jnp.zeros((tc, GROUP), F32)
    for j in range(CONV_W):
        acc = acc + h_ref[pl.ds(CONV_HALO - (CONV_W - 1) + j, tc), :] * w_ref[j:j + 1, :]
    y = _layernorm(acc + cb_ref[...], g_ref[...], b_ref[...])
    o_ref[...] = (y * jax.nn.sigmoid(y)).astype(o_ref.dtype)


def _conv(pb, conv_w, cb, g, b, B, L, tc):
    nc = L // tc
    r = tc // CONV_HALO
    row = lambda a: a.reshape(1, GROUP)
    return pl.pallas_call(
        functools.partial(_conv_kernel, tc=tc),
        grid=(B, nc),
        in_specs=[pl.BlockSpec((tc, 2 * GROUP), lambda bb, i: (bb * nc + i, 0)),
                  pl.BlockSpec((CONV_HALO, 2 * GROUP), lambda bb, i: (jnp.maximum((bb * nc + i) * r - 1, 0), 0)),
                  pl.BlockSpec((CONV_W, GROUP), lambda bb, i: (0, 0)),
                  pl.BlockSpec((1, GROUP), lambda bb, i: (0, 0)),
                  pl.BlockSpec((1, GROUP), lambda bb, i: (0, 0)),
                  pl.BlockSpec((1, GROUP), lambda bb, i: (0, 0))],
        out_specs=pl.BlockSpec((tc, GROUP), lambda bb, i: (bb * nc + i, 0)),
        out_shape=jax.ShapeDtypeStruct((B * L, GROUP), BF16),
        scratch_shapes=[pltpu.VMEM((tc + CONV_HALO, GROUP), F32)],
        compiler_params=_cparams(("parallel", "arbitrary")),
        name="conv",
    )(pb, pb, conv_w, row(cb), row(g), row(b))


def _mix_kernel(yf_ref, yd_ref, ys_ref, yc_ref, x_ref, wo_ref, g_ref, b_ref, rw_ref, rb_ref,
                x1_ref, idx_ref, wt_ref, *, alpha):
    mix = None
    for n, y_ref in enumerate((yf_ref, yd_ref, ys_ref, yc_ref)):
        part = _dot(y_ref[...], wo_ref[n * GROUP:(n + 1) * GROUP, :])
        mix = part if mix is None else mix + part
    x1 = _layernorm(alpha * x_ref[...] + mix, g_ref[...], b_ref[...])
    x1_ref[...] = x1

    a1, a2, a3 = _split3(x1)
    w1, w2, w3 = rw_ref[0], rw_ref[1], rw_ref[2]
    lg = (_dot(a1, w1) + (_dot(a1, w2) + _dot(a2, w1)) + (_dot(a1, w3) + _dot(a2, w2) + _dot(a3, w1))) + rb_ref[...]

    tm = lg.shape[0]
    lane = lax.broadcasted_iota(I32, (tm, LANE), 1)
    lane_f = lane.astype(F32)
    idx_out = jnp.zeros((tm, LANE), I32)
    vals = []
    for r in range(TOP_K):
        mx = jnp.max(lg, axis=1, keepdims=True)
        am = jnp.min(jnp.where(lg == mx, lane_f, float(LANE)), axis=1, keepdims=True).astype(I32)
        vals.append(mx)
        idx_out = jnp.where(lane == r, am, idx_out)
        lg = jnp.where(lane == am, 2 * NEG, lg)
    es = [jnp.exp(v - vals[0]) for v in vals]
    den = es[0] + es[1] + es[2] + es[3]
    wt = jnp.zeros((tm, LANE), F32)
    for r in range(TOP_K):
        wt = jnp.where(lane == r, es[r] / den, wt)
    idx_ref[...] = idx_out
    wt_ref[...] = wt


def _mix(ys, x2d, wo, g, b, rw3, rb, alpha, tm):
    T = x2d.shape[0]
    yspec = pl.BlockSpec((tm, GROUP), lambda i: (i, 0))
    full = lambda shp: pl.BlockSpec(shp, lambda i: (0,) * len(shp))
    return pl.pallas_call(
        functools.partial(_mix_kernel, alpha=alpha),
        grid=(T // tm,),
        in_specs=[yspec, yspec, yspec, yspec,
                  pl.BlockSpec((tm, D_MODEL), lambda i: (i, 0)),
                  full((D_MODEL, D_MODEL)), full((1, D_MODEL)), full((1, D_MODEL)),
                  full((3, D_MODEL, LANE)), full((1, LANE))],
        out_specs=[pl.BlockSpec((tm, D_MODEL), lambda i: (i, 0)),
                   pl.BlockSpec((tm, LANE), lambda i: (i, 0)),
                   pl.BlockSpec((tm, LANE), lambda i: (i, 0))],
        out_shape=[jax.ShapeDtypeStruct((T, D_MODEL), F32),
                   jax.ShapeDtypeStruct((T, LANE), I32),
                   jax.ShapeDtypeStruct((T, LANE), F32)],
        compiler_params=_cparams(("parallel",)),
        name="mix",
    )(*ys, x2d, wo, g.reshape(1, D_MODEL), b.reshape(1, D_MODEL), rw3, rb)


def _start_row_gather(idx_ref, base, n, src_hbm, dst_ref, sem, inline=False):
    def body(r, _):
        pltpu.make_async_copy(src_hbm.at[pl.ds(idx_ref[base + r], 1)], dst_ref.at[pl.ds(r, 1)], sem).start()
        return 0
    if inline:
        for r in range(n):
            body(r, 0)
    else:
        lax.fori_loop(0, n, body, 0, unroll=8)


def _wait_row_gather(n, src_hbm, dst_ref, sem):
    pltpu.make_async_copy(src_hbm.at[pl.ds(0, n)], dst_ref, sem).wait()


def _expert_kernel(te_ref, tv_ref, tb_ref, tok_ref, x_hbm, wt_ref, bg_ref, bu_ref, wd_ref, bd_ref, y_ref,
                   xbuf, sem, wg_s, wu_s, wd_s, *, tmm):
    i = pl.program_id(0)
    n = pl.num_programs(0)
    slot = lax.rem(i, 2)

    @pl.when(i == 0)
    def _():
        _start_row_gather(tok_ref, tb_ref[0], tmm, x_hbm, xbuf.at[0], sem.at[0])

    _wait_row_gather(tmm, x_hbm, xbuf.at[slot], sem.at[slot])
    prefetch = functools.partial(_start_row_gather, tok_ref, tb_ref[i + 1], tmm, x_hbm, xbuf.at[1 - slot],
                                 sem.at[1 - slot])

    @pl.when((tv_ref[i] > 0) & ((i == 0) | (te_ref[i] != te_ref[jnp.maximum(i - 1, 0)])))
    def _():
        for c in range(D_MODEL // LANE):
            rows = slice(c * LANE, (c + 1) * LANE)
            wg_s[rows, :] = wt_ref[0, c, pl.ds(0, D_MODEL, stride=2), :].T.astype(BF16)
            wu_s[rows, :] = wt_ref[0, c, pl.ds(1, D_MODEL, stride=2), :].T.astype(BF16)
        wd_s[...] = wd_ref[0].astype(BF16)

    @pl.when(tv_ref[i] > 0)
    def _():
        prefetch(inline=True)
        xb = xbuf[slot].astype(BF16)
        gate = jnp.minimum(_dot(xb, wg_s[...]) + bg_ref[0], SWIGLU_LIMIT)
        up = jnp.clip(_dot(xb, wu_s[...]) + bu_ref[0], -SWIGLU_LIMIT, SWIGLU_LIMIT)
        h = (up + 1.0) * gate * jax.nn.sigmoid(gate * SWIGLU_ALPHA)
        y_ref[...] = _dot(h.astype(BF16), wd_s[...]) + bd_ref[0]

    @pl.when(tv_ref[i] == 0)
    def _():
        prefetch()
        y_ref[...] = jnp.zeros_like(y_ref)

    @pl.when(i == n - 1)
    def _():
        _wait_row_gather(tmm, x_hbm, xbuf.at[1 - slot], sem.at[1 - slot])


def _experts(tile_e, tile_v, tile_base, tok, x1, wgu_t, bg, bu, wd, bd, R, tmm):
    by_expert = lambda shp: pl.BlockSpec((1,) + shp, lambda i, te, tv, tb, tk: (te[i],) + (0,) * len(shp))
    wscratch = pltpu.VMEM((D_MODEL, D_MODEL), BF16)
    return pl.pallas_call(
        functools.partial(_expert_kernel, tmm=tmm),
        grid_spec=pltpu.PrefetchScalarGridSpec(
            num_scalar_prefetch=4,
            grid=(R // tmm,),
            in_specs=[pl.BlockSpec(memory_space=pl.ANY),
                      by_expert((D_MODEL // LANE, 2 * D_MODEL, LANE)), by_expert((1, D_MODEL)),
                      by_expert((1, D_MODEL)),
                      by_expert((D_MODEL, D_MODEL)), by_expert((1, D_MODEL))],
            out_specs=pl.BlockSpec((tmm, D_MODEL), lambda i, te, tv, tb, tk: (i, 0)),
            scratch_shapes=[pltpu.VMEM((2, tmm, D_MODEL), F32), pltpu.SemaphoreType.DMA((2,)),
                            wscratch, wscratch, wscratch],
        ),
        out_shape=jax.ShapeDtypeStruct((R, D_MODEL), F32),
        compiler_params=_cparams(("arbitrary",)),
        name="moe_experts",
    )(tile_e, tile_v, tile_base, tok, x1, wgu_t, bg, bu, wd, bd)


def _combine_kernel(pos_ref, ys_hbm, x1_ref, wt_ref, g_ref, b_ref, o_ref, buf, sem, *, tc, alpha):
    i = pl.program_id(0)
    n = pl.num_programs(0)
    slot = lax.rem(i, 2)
    rows = tc * TOP_K

    @pl.when(i == 0)
    def _():
        _start_row_gather(pos_ref, 0, rows, ys_hbm, buf.at[0], sem.at[0])

    @pl.when(i + 1 < n)
    def _():
        _start_row_gather(pos_ref, (i + 1) * rows, rows, ys_hbm, buf.at[1 - slot], sem.at[1 - slot], inline=True)

    _wait_row_gather(rows, ys_hbm, buf.at[slot], sem.at[slot])
    wt = wt_ref[...]
    ffn = buf[slot, pl.ds(0, tc), :] * wt[:, 0:1]
    for k in range(1, TOP_K):
        ffn = ffn + buf[slot, pl.ds(k * tc, tc), :] * wt[:, k:k + 1]
    o_ref[...] = _layernorm(alpha * x1_ref[...] + ffn, g_ref[...], b_ref[...])


def _combine(pos, ys, x1, wt, g, b, alpha, tc):
    T = x1.shape[0]
    pos_t = jnp.transpose(pos.reshape(T // tc, tc, TOP_K), (0, 2, 1)).reshape(T * TOP_K)
    return pl.pallas_call(
        functools.partial(_combine_kernel, tc=tc, alpha=alpha),
        grid_spec=pltpu.PrefetchScalarGridSpec(
            num_scalar_prefetch=1,
            grid=(T // tc,),
            in_specs=[pl.BlockSpec(memory_space=pl.ANY),
                      pl.BlockSpec((tc, D_MODEL), lambda i, p: (i, 0)),
                      pl.BlockSpec((tc, LANE), lambda i, p: (i, 0)),
                      pl.BlockSpec((1, D_MODEL), lambda i, p: (0, 0)),
                      pl.BlockSpec((1, D_MODEL), lambda i, p: (0, 0))],
            out_specs=pl.BlockSpec((tc, D_MODEL), lambda i, p: (i, 0)),
            scratch_shapes=[pltpu.VMEM((2, TOP_K * tc, D_MODEL), F32), pltpu.SemaphoreType.DMA((2,))],
        ),
        out_shape=jax.ShapeDtypeStruct((T, D_MODEL), F32),
        compiler_params=_cparams(("arbitrary",)),
        name="moe_combine",
    )(pos_t, ys, x1, wt, g.reshape(1, D_MODEL), b.reshape(1, D_MODEL))


def _moe_plan(top_idx, tmm):
    T = top_idx.shape[0]
    F = T * TOP_K
    R = F + N_EXPERTS * tmm
    flat_e = top_idx.reshape(F)
    onehot = (flat_e[:, None] == jnp.arange(N_EXPERTS, dtype=I32)[None, :]).astype(I32)
    csum = jnp.cumsum(onehot, axis=0)
    counts = csum[-1]
    padded = ((counts + tmm - 1) // tmm) * tmm
    gend = jnp.cumsum(padded)
    gstart = gend - padded
    ustart = jnp.cumsum(counts) - counts
    pos = jnp.sum(onehot * (csum - 1 + gstart[None, :]), axis=1)
    order = jnp.argsort(flat_e, stable=True).astype(I32)
    n_tiles = R // tmm + 1
    t_start = jnp.arange(n_tiles, dtype=I32) * tmm
    t_e = jnp.minimum(jnp.sum((t_start[:, None] >= gend[None, :]).astype(I32), axis=1), N_EXPERTS - 1)
    tile_base = jnp.clip(ustart[t_e] + t_start - gstart[t_e], 0, F)
    tok = jnp.concatenate([order // TOP_K, jnp.zeros((tmm,), I32)])
    tile_v = (t_start[:-1] < gend[-1]).astype(I32)
    last_e = jnp.sum((gend[-1] - 1 >= gend).astype(I32))
    tile_e = jnp.where(tile_v > 0, t_e[:-1], last_e)
    return pos.astype(I32), tok, tile_base.astype(I32), tile_e.astype(I32), tile_v, R


def _rel_bucket(dist):
    n = jnp.maximum(dist, 0)
    max_exact = NUM_BUCKETS // 2
    nf = jnp.maximum(n, 1).astype(F32)
    large = max_exact + (jnp.log(nf / max_exact) / math.log(MAX_DISTANCE / max_exact)
                         * (NUM_BUCKETS - max_exact)).astype(I32)
    large = jnp.minimum(large, NUM_BUCKETS - 1)
    return jnp.where(n < max_exact, n, large)


def _bias_tables(table, t):
    assert t >= MAX_DISTANCE, "key blocks two or more away must lie entirely in the last bucket"
    heads = table.shape[1]
    rel = (table - table[NUM_BUCKETS - 1][None, :]) * LOG2E
    dist = jnp.arange(-t, 2 * t, dtype=I32)
    bucket = _rel_bucket(dist)[None]
    line = jnp.zeros((heads, 3 * t), F32)
    for n in range(NUM_BUCKETS):
        line = jnp.where(bucket == n, rel[n][:, None], line)
    line = jnp.where((dist >= 0)[None], line, NEG)

    def toeplitz(c):
        flat = jnp.tile(c, (1, t))[:, :t * (2 * t - 1)]
        return flat.reshape(heads, t, 2 * t - 1)[:, :, :t]

    diag = toeplitz(jnp.concatenate([line[:, t:2 * t], line[:, 0:t]], axis=1))
    sub = toeplitz(jnp.concatenate([line[:, 2 * t:3 * t], line[:, t:2 * t]], axis=1))
    return jnp.stack([diag, sub], axis=1)


def kernel(x, rel_bias, w_in, forget_b, diff_lambda, diff_norm_g, conv_w, conv_b, conv_ln_g, conv_ln_b,
           w_out, ln1_g, ln1_b, router_w, router_b, w_gu, b_gu, w_down, b_down, ln2_g, ln2_b):
    B, L, D = x.shape
    depth = w_in.shape[0]
    assert D == D_MODEL
    T = B * L
    t = 512 if L % 512 == 0 else 256
    tm = 256
    tmm = 256
    tcv = 512 if L % 512 == 0 else 256
    assert L % t == 0 and T % tm == 0
    top_k = min(K_SEL_MAX, L // 4)
    alpha = (2 * depth) ** 0.25

    dtab_diff = _bias_tables(rel_bias[:, :N_HEADS], t)
    dtab_dsa = _bias_tables(rel_bias[:, N_HEADS:], t)

    x2d = x.reshape(T, D)
    for l in range(depth):
        lambda_init = 0.8 - 0.6 * math.exp(-0.3 * l)
        qt, ka, vt, pb, gt = _proj(x2d, _proj_weights(w_in[l]), t)

        fb_row = jnp.zeros((1, LANE), F32).at[0, :N_HEADS].set(forget_b[l])
        caug = _foxgate(pb, fb_row, B, L)
        y_fox = _fox(qt, ka, caug, vt, B, L, t)

        lp = diff_lambda[l].astype(F32)
        lam = (jnp.exp(jnp.sum(lp[0] * lp[1])) - jnp.exp(jnp.sum(lp[2] * lp[3])) + lambda_init).reshape(1)
        g_tile = jnp.broadcast_to(diff_norm_g[l].astype(F32)[:, None], (HEAD_DIM, t))
        y_diff = _diff(lam, qt, ka, vt, dtab_diff, g_tile, B, L, t, 1.0 - lambda_init)

        y_dsa = _dsa(qt, ka, vt, gt, dtab_dsa, B, L, t, top_k)
        y_conv = _conv(pb, conv_w[l], conv_b[l], conv_ln_g[l], conv_ln_b[l], B, L, tcv)

        rw = jnp.zeros((D, LANE), F32).at[:, :N_EXPERTS].set(router_w[l])
        r1 = rw.astype(BF16)
        r2 = (rw - r1.astype(F32)).astype(BF16)
        r3 = (rw - r1.astype(F32) - r2.astype(F32)).astype(BF16)
        rb = jnp.full((1, LANE), NEG, F32).at[0, :N_EXPERTS].set(router_b[l])
        x1, top_idx, top_w = _mix((y_fox, y_diff, y_dsa, y_conv), x2d, w_out[l].astype(BF16),
                                  ln1_g[l], ln1_b[l], jnp.stack([r1, r2, r3]), rb, alpha, tm)

        pos, tok, tile_base, tile_e, tile_v, R = _moe_plan(top_idx[:, :TOP_K], tmm)
        wgu_t = jnp.transpose(w_gu[l].reshape(N_EXPERTS, D // LANE, LANE, 2 * D), (0, 1, 3, 2))
        ys = _experts(tile_e, tile_v, tile_base, tok, x1, wgu_t, b_gu[l][:, None, 0::2],
                      b_gu[l][:, None, 1::2], w_down[l], b_down[l][:, None, :], R, tmm)
        x2d = _combine(pos, ys, x1, top_w, ln2_g[l], ln2_b[l], alpha, 128)
    return x2d.reshape(B, L, D)
```

```python
import functools
import math

import numpy as np
import jax
import jax.numpy as jnp
from jax import lax
from jax.experimental import pallas as pl
from jax.experimental.pallas import tpu as pltpu

F32 = jnp.float32
BF16 = jnp.bfloat16
I32 = jnp.int32

D_MODEL = 1024
HEAD_DIM = 64
N_HEADS = 4
GROUP = N_HEADS * HEAD_DIM
DIFF_QK = HEAD_DIM // 2
D_IDX = 64
K_SEL_MAX = 256
CONV_W = 31
CONV_HALO = 32
NUM_BUCKETS = 32
MAX_DISTANCE = 128
N_EXPERTS = 32
TOP_K = 4
SWIGLU_LIMIT = 7.0
SWIGLU_ALPHA = 1.702
LN_EPS = 1e-5
LOG2E = math.log2(math.e)
NEG = -1e30
INT_MIN = -(2 ** 31)
INT_MAX = 2 ** 31 - 1
LANE = 128
VMEM_LIMIT = 56 * 1024 * 1024

_IN_WIDTHS = (GROUP, GROUP, GROUP, N_HEADS, GROUP, GROUP, GROUP, GROUP, GROUP, GROUP,
              N_HEADS * D_IDX, D_IDX, N_HEADS, 2 * GROUP)
_IN_OFF = tuple(int(v) for v in np.cumsum((0,) + _IN_WIDTHS))
(_FQ, _FK, _FV, _FF, _DQ, _DK, _DV, _SQ, _SK, _SV, _IQ, _IK, _IW, _CU) = range(14)

QT_GROUP = {"f": 0, "d": 1, "s": 2, "i": 3}
KA_BLOCK = {"f": 0, "d": 1, "s": 2}
NK = 3 * GROUP + LANE
IK_BLOCK = 3 * GROUP // LANE
NB = 2 * GROUP + LANE
GATE_BLOCK = 2 * GROUP // LANE
GT_ROWS = 16


def _cols(w, slots, scales=None):
    parts = []
    for n, s in enumerate(slots):
        part = w[:, _IN_OFF[s]:_IN_OFF[s + 1]]
        parts.append(part if scales is None or scales[n] == 1.0 else part * scales[n])
    return jnp.concatenate(parts, axis=1)


def _proj_weights(w):
    qs = HEAD_DIM ** -0.5 * LOG2E
    iw_scale = N_HEADS ** -0.5
    wq_t = _cols(w, (_FQ, _DQ, _SQ, _IQ), (qs, DIFF_QK ** -0.5 * LOG2E, qs, D_IDX ** -0.5)).T.astype(BF16)
    wk = jnp.pad(_cols(w, (_FK, _DK, _SK, _IK)), ((0, 0), (0, LANE - D_IDX))).astype(BF16)
    wv_t = _cols(w, (_FV, _DV, _SV)).T.astype(BF16)
    wb = jnp.pad(_cols(w, (_CU, _FF, _IW), (1.0, 1.0, iw_scale)),
                 ((0, 0), (0, LANE - 2 * N_HEADS))).astype(BF16)
    wg_t = jnp.pad((w[:, _IN_OFF[_IW]:_IN_OFF[_IW + 1]] * iw_scale).T,
                   ((0, GT_ROWS - N_HEADS), (0, 0))).astype(BF16)
    return wq_t, wk, wv_t, wb, wg_t


def _cparams(sem):
    return pltpu.CompilerParams(dimension_semantics=sem, vmem_limit_bytes=VMEM_LIMIT)


def _dot(a, b):
    return jnp.dot(a, b, preferred_element_type=F32)


def _dot_nt(a, b):
    return lax.dot_general(a, b, (((1,), (1,)), ((), ())), preferred_element_type=F32)


def _split3(a):
    a1 = a.astype(BF16)
    r1 = a - a1.astype(F32)
    a2 = r1.astype(BF16)
    a3 = (r1 - a2.astype(F32)).astype(BF16)
    return a1, a2, a3


def _layernorm(z, g, b):
    mu = jnp.mean(z, axis=-1, keepdims=True)
    zc = z - mu
    var = jnp.mean(zc * zc, axis=-1, keepdims=True)
    return zc * lax.rsqrt(var + LN_EPS) * g + b


def _proj_kernel(x_ref, wq_ref, wk_ref, wv_ref, wb_ref, wg_ref, qt_ref, ka_ref, vt_ref, pb_ref, gt_ref):
    xb = x_ref[...].astype(BF16)
    qt_ref[...] = _dot_nt(wq_ref[...], xb).astype(qt_ref.dtype)
    ka_ref[...] = _dot(xb, wk_ref[...]).astype(ka_ref.dtype)
    vt = _dot_nt(wv_ref[...], xb)
    for g in range(3):
        vt_ref[g, 0] = vt[g * GROUP:(g + 1) * GROUP, :].astype(vt_ref.dtype)
    pb_ref[...] = _dot(xb, wb_ref[...])
    gt_ref[...] = _dot_nt(wg_ref[...], xb)


def _proj(x2d, weights, t):
    T = x2d.shape[0]
    full = lambda a: pl.BlockSpec(a.shape, lambda i: (0,) * a.ndim)
    return pl.pallas_call(
        _proj_kernel,
        grid=(T // t,),
        in_specs=[pl.BlockSpec((t, D_MODEL), lambda i: (i, 0))] + [full(a) for a in weights],
        out_specs=[pl.BlockSpec((4 * GROUP, t), lambda i: (0, i)),
                   pl.BlockSpec((t, NK), lambda i: (i, 0)),
                   pl.BlockSpec((3, 1, GROUP, t), lambda i: (0, i, 0, 0)),
                   pl.BlockSpec((t, NB), lambda i: (i, 0)),
                   pl.BlockSpec((GT_ROWS, t), lambda i: (0, i))],
        out_shape=[jax.ShapeDtypeStruct((4 * GROUP, T), BF16),
                   jax.ShapeDtypeStruct((T, NK), BF16),
                   jax.ShapeDtypeStruct((3, T // t, GROUP, t), BF16),
                   jax.ShapeDtypeStruct((T, NB), F32),
                   jax.ShapeDtypeStruct((GT_ROWS, T), F32)],
        compiler_params=_cparams(("parallel",)),
        name="proj",
    )(x2d, *weights)


def _foxgate_kernel(g_ref, fb_ref, c_ref, *, chunk):
    L = g_ref.shape[0]
    row = lax.broadcasted_iota(I32, (chunk, chunk), 0)
    col = lax.broadcasted_iota(I32, (chunk, chunk), 1)
    tri = jnp.where(col <= row, 1.0, 0.0).astype(BF16)
    lane = lax.broadcasted_iota(I32, (chunk, LANE), 1)

    def body(i, carry):
        r0 = pl.multiple_of(i * chunk, chunk)
        z = g_ref[pl.ds(r0, chunk), :] + fb_ref[...]
        lf = jnp.minimum(z, 0.0) - jnp.log1p(jnp.exp(-jnp.abs(z)))
        p1, p2, p3 = _split3(lf)
        cs = _dot(tri, p1) + _dot(tri, p2) + _dot(tri, p3) + carry
        cl = jnp.where(lane < N_HEADS, cs * LOG2E, 0.0)
        c1 = cl.astype(BF16).astype(F32)
        r1 = cl - c1
        c2 = r1.astype(BF16).astype(F32)
        c3 = r1 - c2
        out = c1 + pltpu.roll(c2, N_HEADS, axis=1) + pltpu.roll(c3, 2 * N_HEADS, axis=1)
        c_ref[pl.ds(r0, chunk), :] = out.astype(c_ref.dtype)
        return cs[chunk - 1:chunk, :]

    lax.fori_loop(0, L // chunk, body, jnp.zeros((1, LANE), F32))


def _foxgate(pb, fb_row, B, L):
    return pl.pallas_call(
        functools.partial(_foxgate_kernel, chunk=256),
        grid=(B,),
        in_specs=[pl.BlockSpec((L, LANE), lambda b: (b, GATE_BLOCK)),
                  pl.BlockSpec((1, LANE), lambda b: (0, 0))],
        out_specs=pl.BlockSpec((L, LANE), lambda b: (b, 0)),
        out_shape=jax.ShapeDtypeStruct((B * L, LANE), BF16),
        compiler_params=_cparams(("parallel",)),
        name="foxgate",
    )(pb, fb_row)


def _online_update(carry, s, vt):
    m, l, acc = carry
    m_new = jnp.maximum(m, jnp.max(s, axis=0, keepdims=True))
    alpha = jnp.exp2(m - m_new)
    p = jnp.exp2(s - m_new)
    l = alpha * l + jnp.sum(p, axis=0, keepdims=True)
    acc = alpha * acc + _dot(vt, p.astype(BF16))
    return m_new, l, acc


def _softmax_init(t):
    return (jnp.full((1, t), NEG, F32), jnp.zeros((1, t), F32), jnp.zeros((HEAD_DIM, t), F32))


def _causal(t):
    return lax.broadcasted_iota(I32, (t, t), 0) <= lax.broadcasted_iota(I32, (t, t), 1)


def _pair_rows(q_ref, h):
    p = h // 2
    return q_ref[p * LANE:(p + 1) * LANE, :]


def _q_spec(L, t, group):
    nq = L // t
    return pl.BlockSpec((GROUP, t), lambda b, i: (group, b * nq + i))


def _k_spec(L, block):
    return pl.BlockSpec((L, GROUP), lambda b, i: (b, block))


def _v_spec(L, t, group):
    return pl.BlockSpec((1, L // t, GROUP, t), lambda b, i: (group, b, 0, 0))


def _out_spec(L, t):
    nq = L // t
    return pl.BlockSpec((t, GROUP), lambda b, i: (b * nq + i, 0))


def _fox_kernel(q_ref, k_ref, ca_ref, v_ref, o_ref, qa_ref, *, t):
    i = pl.program_id(1)
    row = lax.broadcasted_iota(I32, (LANE, t), 0)
    for h in range(N_HEADS):
        qz = jnp.where((row // HEAD_DIM) == (h % 2), _pair_rows(q_ref, h), 0)
        sel = jnp.where((row == h) | (row == N_HEADS + h) | (row == 2 * N_HEADS + h), -1.0, 0.0).astype(BF16)
        qa_ref[h // 2, :, (h % 2) * t:(h % 2 + 1) * t] = jnp.concatenate([qz.astype(BF16), sel], axis=0)
    keep = _causal(t)

    def step(j, carry, masked):
        k0 = pl.multiple_of(j * t, t)
        kk = k_ref[pl.ds(k0, t), :]
        ca = ca_ref[pl.ds(k0, t), :]
        vt = v_ref[0, j]
        sp = [_dot(jnp.concatenate([kk[:, p * LANE:(p + 1) * LANE], ca], axis=1), qa_ref[p])
              for p in range(N_HEADS // 2)]
        out = []
        for h in range(N_HEADS):
            s = sp[h // 2][:, (h % 2) * t:(h % 2 + 1) * t]
            if masked:
                s = jnp.where(keep, s, NEG)
            out.append(_online_update(carry[h], s, vt[h * HEAD_DIM:(h + 1) * HEAD_DIM, :]))
        return tuple(out)

    carry = lax.fori_loop(0, i, functools.partial(step, masked=False),
                          tuple(_softmax_init(t) for _ in range(N_HEADS)))
    carry = step(i, carry, True)
    ot = jnp.concatenate([acc / l for (_, l, acc) in carry], axis=0)
    o_ref[...] = ot.T.astype(o_ref.dtype)


def _fox(qt, ka, caug, vt, B, L, t):
    return pl.pallas_call(
        functools.partial(_fox_kernel, t=t),
        grid=(B, L // t),
        in_specs=[_q_spec(L, t, QT_GROUP["f"]), _k_spec(L, KA_BLOCK["f"]),
                  pl.BlockSpec((L, LANE), lambda b, i: (b, 0)), _v_spec(L, t, 0)],
        out_specs=_out_spec(L, t),
        out_shape=jax.ShapeDtypeStruct((B * L, GROUP), BF16),
        scratch_shapes=[pltpu.VMEM((N_HEADS // 2, 2 * LANE, 2 * t), BF16)],
        compiler_params=_cparams(("parallel", "arbitrary")),
        name="fox",
    )(qt, ka, caug, vt)


def _diff_kernel(lam_ref, q_ref, k_ref, v_ref, dtab_ref, g_ref, o_ref, qz_ref, *, t, out_scale):
    i = pl.program_id(1)
    lam = lam_ref[0]
    row = lax.broadcasted_iota(I32, (LANE, t), 0)
    for h in range(N_HEADS):
        for m in range(2):
            lo = (h % 2) * HEAD_DIM + m * DIFF_QK
            n = 2 * (h % 2) + m
            qz_ref[h // 2, :, n * t:(n + 1) * t] = jnp.where((row >= lo) & (row < lo + DIFF_QK),
                                                            _pair_rows(q_ref, h), 0).astype(BF16)

    pieces = []
    for p in range(N_HEADS // 2):
        def step(j, carry, near, p=p):
            k0 = pl.multiple_of(j * t, t)
            vt = v_ref[0, j]
            sp = _dot(k_ref[pl.ds(k0, t), p * LANE:(p + 1) * LANE], qz_ref[p])
            out = []
            for hh in range(2):
                h = 2 * p + hh
                for m in range(2):
                    n = 2 * hh + m
                    s = sp[:, n * t:(n + 1) * t]
                    if near:
                        s = s + dtab_ref[h, i - j]
                    out.append(_online_update(carry[2 * hh + m], s, vt[h * HEAD_DIM:(h + 1) * HEAD_DIM, :]))
            return tuple(out)

        n_far = jnp.maximum(i - 1, 0)
        carry = lax.fori_loop(0, n_far, functools.partial(step, near=False),
                              tuple(_softmax_init(t) for _ in range(4)))
        carry = lax.fori_loop(n_far, i + 1, functools.partial(step, near=True), carry)
        for hh in range(2):
            (_, l1, a1), (_, l2, a2) = carry[2 * hh], carry[2 * hh + 1]
            o = a1 / l1 - lam * (a2 / l2)
            o = o * lax.rsqrt(jnp.mean(o * o, axis=0, keepdims=True) + LN_EPS)
            pieces.append(o * g_ref[...] * out_scale)
    o_ref[...] = jnp.concatenate(pieces, axis=0).T.astype(o_ref.dtype)


def _diff(lam, qt, ka, vt, dtab, g_tile, B, L, t, out_scale):
    return pl.pallas_call(
        functools.partial(_diff_kernel, t=t, out_scale=out_scale),
        grid=(B, L // t),
        in_specs=[pl.BlockSpec(memory_space=pltpu.SMEM),
                  _q_spec(L, t, QT_GROUP["d"]), _k_spec(L, KA_BLOCK["d"]), _v_spec(L, t, 1),
                  pl.BlockSpec((N_HEADS, 2, t, t), lambda b, i: (0, 0, 0, 0)),
                  pl.BlockSpec((HEAD_DIM, t), lambda b, i: (0, 0))],
        out_specs=_out_spec(L, t),
        out_shape=jax.ShapeDtypeStruct((B * L, GROUP), BF16),
        scratch_shapes=[pltpu.VMEM((N_HEADS // 2, LANE, 4 * t), BF16)],
        compiler_params=_cparams(("parallel", "arbitrary")),
        name="diff",
    )(lam, qt, ka, vt, dtab, g_tile)


def _floor_avg(a, b):
    return (a >> 1) + (b >> 1) + (a & b & 1)


def _dsa_kernel(q_ref, k_ref, v_ref, iq_ref, ik_ref, iw_ref, dtab_ref, o_ref, keys_ref, qz_ref, iqz_ref, gm_ref,
                *, t, top_k, L):
    i = pl.program_id(1)
    nblk = i + 1
    keep = _causal(t)
    key_row = lax.broadcasted_iota(I32, (t, t), 0)
    q_pos = i * t + lax.broadcasted_iota(I32, (1, t), 1)
    row = lax.broadcasted_iota(I32, (LANE, t), 0)
    for h in range(N_HEADS):
        qz_ref[h // 2, :, (h % 2) * t:(h % 2 + 1) * t] = jnp.where(
            (row // HEAD_DIM) == (h % 2), _pair_rows(q_ref, h), 0).astype(BF16)
        iqz_ref[:, h * t:(h + 1) * t] = jnp.concatenate(
            [iq_ref[h * D_IDX:(h + 1) * D_IDX, :], jnp.zeros((LANE - D_IDX, t), BF16)], axis=0)

    iw = iw_ref[...]

    gm_ref[...] = jnp.full((t, t), INT_MIN, I32)

    def score_block(c, _, masked):
        k0 = pl.multiple_of(c * t, t)
        a = _dot(ik_ref[pl.ds(k0, t), :], iqz_ref[...])
        acc = None
        for h in range(N_HEADS):
            term = jnp.maximum(a[:, h * t:(h + 1) * t], 0.0) * iw[h:h + 1, :]
            acc = term if acc is None else acc + term
        bits = lax.bitcast_convert_type(acc, I32)
        key = jnp.where(bits < 0, bits ^ INT_MAX, bits)
        key = key - (key >> 31)
        if masked:
            key = jnp.where(keep, key, INT_MIN)
        keys_ref[c] = key
        gm_ref[...] = jnp.maximum(gm_ref[...], key)
        return 0

    lax.fori_loop(0, i, functools.partial(score_block, masked=False), 0)
    score_block(i, 0, True)

    g = gm_ref[...]
    while g.shape[0] // 2 >= top_k:
        half = g.shape[0] // 2
        g = jnp.maximum(g[:half], g[half:])
    assert g.shape[0] >= top_k

    def fold_rows(x, op):
        return op(op(x.reshape(x.shape[0] // 8, 8, t), axis=0).astype(F32), axis=0, keepdims=True)

    margin, edge = 256.0, 2.0 ** 31 - 1024.0
    lo_b = jnp.maximum(fold_rows(g, jnp.min) - margin, -edge).astype(I32)
    hi_b = jnp.minimum(fold_rows(g, jnp.max) + margin, edge).astype(I32)

    def count(pred):
        def body(c, acc):
            return acc + jnp.sum(pred(keys_ref[c], c).reshape(t // 8, 8, t), axis=0)
        acc = lax.fori_loop(0, nblk, body, jnp.zeros((8, t), F32))
        return jnp.sum(acc, axis=0, keepdims=True).astype(I32)

    n_valid = q_pos + 1
    small = n_valid <= top_k
    lo0 = jnp.full((1, t), INT_MIN + 1, I32)
    state0 = dict(lo=lo0, hi=hi_b, clo=n_valid, chi=jnp.zeros((1, t), I32),
                  thr=lo0, need=jnp.zeros((1, t), I32), tied=jnp.zeros((1, t), I32), done=small.astype(I32))

    def n_active(done):
        return jnp.sum((1 - done).astype(F32))

    def body(c):
        _, st = c
        lo, hi, clo, chi, done = st["lo"], st["hi"], st["clo"], st["chi"], st["done"]
        mid = _floor_avg(lo, hi)
        mid = jnp.where((lo < 0) & (hi > 0), 0, mid)
        mid = jnp.where((lo == 0) & (hi > 1), 1, mid)
        first = (lo == INT_MIN + 1) & (lo_b > INT_MIN + 1)
        mid = jnp.where(first, jnp.minimum(lo_b, hi - 1), mid)
        cnt = count(lambda kc, _c: jnp.where(kc >= mid, 1.0, 0.0))
        act = done == 0
        ge = cnt >= top_k
        lo_n = jnp.where(ge, mid, lo)
        clo_n = jnp.where(ge, cnt, clo)
        hi_n = jnp.where(ge, hi, mid)
        chi_n = jnp.where(ge, chi, cnt)
        exact = cnt == top_k
        fin = act & (exact | (hi_n == lo_n + 1))
        upd = lambda new, old: jnp.where(act, new, old)
        st_n = dict(lo=upd(lo_n, lo), hi=upd(hi_n, hi), clo=upd(clo_n, clo), chi=upd(chi_n, chi),
                    thr=jnp.where(fin, jnp.where(exact, mid, lo_n), st["thr"]),
                    need=jnp.where(fin, top_k - chi_n, st["need"]),
                    tied=jnp.where(fin, ((~exact) & (clo_n > top_k)).astype(I32), st["tied"]),
                    done=jnp.where(fin, 1, done))
        return n_active(st_n["done"]), st_n

    _, st = lax.while_loop(lambda c: c[0] > 0.5, body, (n_active(state0["done"]), state0))
    thr, need, tied = st["thr"], st["need"], st["tied"] > 0

    @pl.when(jnp.sum(tied.astype(F32)) > 0.5)
    def _():
        tri = jnp.where(lax.broadcasted_iota(I32, (t, t), 1) <= key_row, 1.0, 0.0).astype(BF16)
        need_f = jnp.where(tied, need, 2 * L).astype(F32)

        def fbody(c, before):
            kc = keys_ref[c]
            e = jnp.where(kc == thr, 1.0, 0.0)
            rank = _dot(tri, e.astype(BF16)) + before
            keys_ref[c] = kc - jnp.where(rank > need_f, e, 0.0).astype(I32)
            return before + jnp.sum(e, axis=0, keepdims=True)

        lax.fori_loop(0, nblk, fbody, jnp.zeros((1, t), F32))

    def attn_block(c, carry, near):
        k0 = pl.multiple_of(c * t, t)
        kk = k_ref[pl.ds(k0, t), :]
        vt = v_ref[0, c]
        sel = keys_ref[c] >= thr
        sp = [_dot(kk[:, p * LANE:(p + 1) * LANE], qz_ref[p]) for p in range(N_HEADS // 2)]
        out = []
        for h in range(N_HEADS):
            s = sp[h // 2][:, (h % 2) * t:(h % 2 + 1) * t]
            if near:
                s = s + dtab_ref[h, i - c]
            s = jnp.where(sel, s, NEG)
            out.append(_online_update(carry[h], s, vt[h * HEAD_DIM:(h + 1) * HEAD_DIM, :]))
        return tuple(out)

    n_far = jnp.maximum(i - 1, 0)
    carry = lax.fori_loop(0, n_far, functools.partial(attn_block, near=False),
                          tuple(_softmax_init(t) for _ in range(N_HEADS)))
    carry = lax.fori_loop(n_far, nblk, functools.partial(attn_block, near=True), carry)
    ot = jnp.concatenate([acc / l for (_, l, acc) in carry], axis=0)
    o_ref[...] = ot.T.astype(o_ref.dtype)


def _dsa(qt, ka, vt, gt, dtab, B, L, t, top_k):
    nq = L // t
    return pl.pallas_call(
        functools.partial(_dsa_kernel, t=t, top_k=top_k, L=L),
        grid=(B, nq),
        in_specs=[_q_spec(L, t, QT_GROUP["s"]), _k_spec(L, KA_BLOCK["s"]), _v_spec(L, t, 2),
                  _q_spec(L, t, QT_GROUP["i"]),
                  pl.BlockSpec((L, LANE), lambda b, i: (b, IK_BLOCK)),
                  pl.BlockSpec((GT_ROWS, t), lambda b, i: (0, b * nq + i)),
                  pl.BlockSpec((N_HEADS, 2, t, t), lambda b, i: (0, 0, 0, 0))],
        out_specs=_out_spec(L, t),
        out_shape=jax.ShapeDtypeStruct((B * L, GROUP), BF16),
        scratch_shapes=[pltpu.VMEM((nq, t, t), I32), pltpu.VMEM((N_HEADS // 2, LANE, 2 * t), BF16),
                        pltpu.VMEM((LANE, N_HEADS * t), BF16), pltpu.VMEM((t, t), I32)],
        compiler_params=_cparams(("parallel", "arbitrary")),
        name="dsa",
    )(qt, ka, vt, qt, ka, gt, dtab)


def _conv_kernel(u_ref, halo_ref, w_ref, cb_ref, g_ref, b_ref, o_ref, h_ref, *, tc):
    i = pl.program_id(1)

    def glu(u):
        return u[:, :GROUP] * jax.nn.sigmoid(u[:, GROUP:])

    h_ref[pl.ds(CONV_HALO, tc), :] = glu(u_ref[...])
    h_ref[pl.ds(0, CONV_HALO), :] = jnp.where(i > 0, glu(halo_ref[...]), 0.0)
    acc = jnp.zeros((tc, GROUP), F32)
    for j in range(CONV_W):
        acc = acc + h_ref[pl.ds(CONV_HALO - (CONV_W - 1) + j, tc), :] * w_ref[j:j + 1, :]
    y = _layernorm(acc + cb_ref[...], g_ref[...], b_ref[...])
    o_ref[...] = (y * jax.nn.sigmoid(y)).astype(o_ref.dtype)


def _conv(pb, conv_w, cb, g, b, B, L, tc):
    nc = L // tc
    r = tc // CONV_HALO
    row = lambda a: a.reshape(1, GROUP)
    return pl.pallas_call(
        functools.partial(_conv_kernel, tc=tc),
        grid=(B, nc),
        in_specs=[pl.BlockSpec((tc, 2 * GROUP), lambda bb, i: (bb * nc + i, 0)),
                  pl.BlockSpec((CONV_HALO, 2 * GROUP), lambda bb, i: (jnp.maximum((bb * nc + i) * r - 1, 0), 0)),
                  pl.BlockSpec((CONV_W, GROUP), lambda bb, i: (0, 0)),
                  pl.BlockSpec((1, GROUP), lambda bb, i: (0, 0)),
                  pl.BlockSpec((1, GROUP), lambda bb, i: (0, 0)),
                  pl.BlockSpec((1, GROUP), lambda bb, i: (0, 0))],
        out_specs=pl.BlockSpec((tc, GROUP), lambda bb, i: (bb * nc + i, 0)),
        out_shape=jax.ShapeDtypeStruct((B * L, GROUP), BF16),
        scratch_shapes=[pltpu.VMEM((tc + CONV_HALO, GROUP), F32)],
        compiler_params=_cparams(("parallel", "arbitrary")),
        name="conv",
    )(pb, pb, conv_w, row(cb), row(g), row(b))


def _mix_kernel(yf_ref, yd_ref, ys_ref, yc_ref, x_ref, wo_ref, g_ref, b_ref, rw_ref, rb_ref,
                x1_ref, idx_ref, wt_ref, *, alpha):
    mix = None
    for n, y_ref in enumerate((yf_ref, yd_ref, ys_ref, yc_ref)):
        part = _dot(y_ref[...], wo_ref[n * GROUP:(n + 1) * GROUP, :])
        mix = part if mix is None else mix + part
    x1 = _layernorm(alpha * x_ref[...] + mix, g_ref[...], b_ref[...])
    x1_ref[...] = x1

    a1, a2, a3 = _split3(x1)
    w1, w2, w3 = rw_ref[0], rw_ref[1], rw_ref[2]
    lg = (_dot(a1, w1) + (_dot(a1, w2) + _dot(a2, w1)) + (_dot(a1, w3) + _dot(a2, w2) + _dot(a3, w1))) + rb_ref[...]

    tm = lg.shape[0]
    lane = lax.broadcasted_iota(I32, (tm, LANE), 1)
    lane_f = lane.astype(F32)
    idx_out = jnp.zeros((tm, LANE), I32)
    vals = []
    for r in range(TOP_K):
        mx = jnp.max(lg, axis=1, keepdims=True)
        am = jnp.min(jnp.where(lg == mx, lane_f, float(LANE)), axis=1, keepdims=True).astype(I32)
        vals.append(mx)
        idx_out = jnp.where(lane == r, am, idx_out)
        lg = jnp.where(lane == am, 2 * NEG, lg)
    es = [jnp.exp(v - vals[0]) for v in vals]
    den = es[0] + es[1] + es[2] + es[3]
    wt = jnp.zeros((tm, LANE), F32)
    for r in range(TOP_K):
        wt = jnp.where(lane == r, es[r] / den, wt)
    idx_ref[...] = idx_out
    wt_ref[...] = wt


def _mix(ys, x2d, wo, g, b, rw3, rb, alpha, tm):
    T = x2d.shape[0]
    yspec = pl.BlockSpec((tm, GROUP), lambda i: (i, 0))
    full = lambda shp: pl.BlockSpec(shp, lambda i: (0,) * len(shp))
    return pl.pallas_call(
        functools.partial(_mix_kernel, alpha=alpha),
        grid=(T // tm,),
        in_specs=[yspec, yspec, yspec, yspec,
                  pl.BlockSpec((tm, D_MODEL), lambda i: (i, 0)),
                  full((D_MODEL, D_MODEL)), full((1, D_MODEL)), full((1, D_MODEL)),
                  full((3, D_MODEL, LANE)), full((1, LANE))],
        out_specs=[pl.BlockSpec((tm, D_MODEL), lambda i: (i, 0)),
                   pl.BlockSpec((tm, LANE), lambda i: (i, 0)),
                   pl.BlockSpec((tm, LANE), lambda i: (i, 0))],
        out_shape=[jax.ShapeDtypeStruct((T, D_MODEL), F32),
                   jax.ShapeDtypeStruct((T, LANE), I32),
                   jax.ShapeDtypeStruct((T, LANE), F32)],
        compiler_params=_cparams(("parallel",)),
        name="mix",
    )(*ys, x2d, wo, g.reshape(1, D_MODEL), b.reshape(1, D_MODEL), rw3, rb)


def _start_row_gather(idx_ref, base, n, src_hbm, dst_ref, sem, inline=False):
    def body(r, _, priority=0):
        pltpu.make_async_copy(src_hbm.at[pl.ds(idx_ref[base + r], 1)], dst_ref.at[pl.ds(r, 1)],
                              sem).start(priority=priority)
        return 0
    if inline:
        for r in range(n):
            body(r, 0, priority=r % 2)
    else:
        lax.fori_loop(0, n, body, 0, unroll=8)


def _wait_row_gather(n, src_hbm, dst_ref, sem):
    pltpu.make_async_copy(src_hbm.at[pl.ds(0, n)], dst_ref, sem).wait()


def _expert_kernel(te_ref, tv_ref, tb_ref, tok_ref, x_hbm, wt_ref, bg_ref, bu_ref, wd_ref, bd_ref, y_ref,
                   xbuf, sem, wg_s, wu_s, wd_s, *, tmm):
    i = pl.program_id(0)
    n = pl.num_programs(0)
    slot = lax.rem(i, 2)

    @pl.when(i == 0)
    def _():
        _start_row_gather(tok_ref, tb_ref[0], tmm, x_hbm, xbuf.at[0], sem.at[0])

    _wait_row_gather(tmm, x_hbm, xbuf.at[slot], sem.at[slot])
    prefetch = functools.partial(_start_row_gather, tok_ref, tb_ref[i + 1], tmm, x_hbm, xbuf.at[1 - slot],
                                 sem.at[1 - slot])

    @pl.when((tv_ref[i] > 0) & ((i == 0) | (te_ref[i] != te_ref[jnp.maximum(i - 1, 0)])))
    def _():
        for c in range(D_MODEL // LANE):
            rows = slice(c * LANE, (c + 1) * LANE)
            wg_s[rows, :] = wt_ref[0, c, pl.ds(0, D_MODEL, stride=2), :].T.astype(BF16)
            wu_s[rows, :] = wt_ref[0, c, pl.ds(1, D_MODEL, stride=2), :].T.astype(BF16)
        wd_s[...] = wd_ref[0].astype(BF16)

    @pl.when(tv_ref[i] > 0)
    def _():
        prefetch(inline=True)
        xb = xbuf[slot].astype(BF16)
        gate = jnp.minimum(_dot(xb, wg_s[...]) + bg_ref[0], SWIGLU_LIMIT)
        up = jnp.clip(_dot(xb, wu_s[...]) + bu_ref[0], -SWIGLU_LIMIT, SWIGLU_LIMIT)
        h = (up + 1.0) * gate * jax.nn.sigmoid(gate * SWIGLU_ALPHA)
        y_ref[...] = _dot(h.astype(BF16), wd_s[...]) + bd_ref[0]

    @pl.when(tv_ref[i] == 0)
    def _():
        prefetch()
        y_ref[...] = jnp.zeros_like(y_ref)

    @pl.when(i == n - 1)
    def _():
        _wait_row_gather(tmm, x_hbm, xbuf.at[1 - slot], sem.at[1 - slot])


def _experts(tile_e, tile_v, tile_base, tok, x1, wgu_t, bg, bu, wd, bd, R, tmm):
    by_expert = lambda shp: pl.BlockSpec((1,) + shp, lambda i, te, tv, tb, tk: (te[i],) + (0,) * len(shp))
    wscratch = pltpu.VMEM((D_MODEL, D_MODEL), BF16)
    return pl.pallas_call(
        functools.partial(_expert_kernel, tmm=tmm),
        grid_spec=pltpu.PrefetchScalarGridSpec(
            num_scalar_prefetch=4,
            grid=(R // tmm,),
            in_specs=[pl.BlockSpec(memory_space=pl.ANY),
                      by_expert((D_MODEL // LANE, 2 * D_MODEL, LANE)), by_expert((1, D_MODEL)),
                      by_expert((1, D_MODEL)),
                      by_expert((D_MODEL, D_MODEL)), by_expert((1, D_MODEL))],
            out_specs=pl.BlockSpec((tmm, D_MODEL), lambda i, te, tv, tb, tk: (i, 0)),
            scratch_shapes=[pltpu.VMEM((2, tmm, D_MODEL), F32), pltpu.SemaphoreType.DMA((2,)),
                            wscratch, wscratch, wscratch],
        ),
        out_shape=jax.ShapeDtypeStruct((R, D_MODEL), F32),
        compiler_params=_cparams(("arbitrary",)),
        name="moe_experts",
    )(tile_e, tile_v, tile_base, tok, x1, wgu_t, bg, bu, wd, bd)


def _combine_kernel(pos_ref, ys_hbm, x1_ref, wt_ref, g_ref, b_ref, o_ref, buf, sem, *, tc, alpha):
    i = pl.program_id(0)
    n = pl.num_programs(0)
    slot = lax.rem(i, 2)
    rows = tc * TOP_K

    @pl.when(i == 0)
    def _():
        _start_row_gather(pos_ref, 0, rows, ys_hbm, buf.at[0], sem.at[0])

    @pl.when(i + 1 < n)
    def _():
        _start_row_gather(pos_ref, (i + 1) * rows, rows, ys_hbm, buf.at[1 - slot], sem.at[1 - slot], inline=True)

    _wait_row_gather(rows, ys_hbm, buf.at[slot], sem.at[slot])
    wt = wt_ref[...]
    ffn = buf[slot, pl.ds(0, tc), :] * wt[:, 0:1]
    for k in range(1, TOP_K):
        ffn = ffn + buf[slot, pl.ds(k * tc, tc), :] * wt[:, k:k + 1]
    o_ref[...] = _layernorm(alpha * x1_ref[...] + ffn, g_ref[...], b_ref[...])


def _combine(pos, ys, x1, wt, g, b, alpha, tc):
    T = x1.shape[0]
    pos_t = jnp.transpose(pos.reshape(T // tc, tc, TOP_K), (0, 2, 1)).reshape(T * TOP_K)
    return pl.pallas_call(
        functools.partial(_combine_kernel, tc=tc, alpha=alpha),
        grid_spec=pltpu.PrefetchScalarGridSpec(
            num_scalar_prefetch=1,
            grid=(T // tc,),
            in_specs=[pl.BlockSpec(memory_space=pl.ANY),
                      pl.BlockSpec((tc, D_MODEL), lambda i, p: (i, 0)),
                      pl.BlockSpec((tc, LANE), lambda i, p: (i, 0)),
                      pl.BlockSpec((1, D_MODEL), lambda i, p: (0, 0)),
                      pl.BlockSpec((1, D_MODEL), lambda i, p: (0, 0))],
            out_specs=pl.BlockSpec((tc, D_MODEL), lambda i, p: (i, 0)),
            scratch_shapes=[pltpu.VMEM((2, TOP_K * tc, D_MODEL), F32), pltpu.SemaphoreType.DMA((2,))],
        ),
        out_shape=jax.ShapeDtypeStruct((T, D_MODEL), F32),
        compiler_params=_cparams(("arbitrary",)),
        name="moe_combine",
    )(pos_t, ys, x1, wt, g.reshape(1, D_MODEL), b.reshape(1, D_MODEL))


def _moe_plan(top_idx, tmm):
    T = top_idx.shape[0]
    F = T * TOP_K
    R = F + N_EXPERTS * tmm
    flat_e = top_idx.reshape(F)
    onehot = (flat_e[:, None] == jnp.arange(N_EXPERTS, dtype=I32)[None, :]).astype(I32)
    csum = jnp.cumsum(onehot, axis=0)
    counts = csum[-1]
    padded = ((counts + tmm - 1) // tmm) * tmm
    gend = jnp.cumsum(padded)
    gstart = gend - padded
    ustart = jnp.cumsum(counts) - counts
    pos = jnp.sum(onehot * (csum - 1 + gstart[None, :]), axis=1)
    order = jnp.argsort(flat_e, stable=True).astype(I32)
    n_tiles = R // tmm + 1
    t_start = jnp.arange(n_tiles, dtype=I32) * tmm
    t_e = jnp.minimum(jnp.sum((t_start[:, None] >= gend[None, :]).astype(I32), axis=1), N_EXPERTS - 1)
    tile_base = jnp.clip(ustart[t_e] + t_start - gstart[t_e], 0, F)
    tok = jnp.concatenate([order // TOP_K, jnp.zeros((tmm,), I32)])
    tile_v = (t_start[:-1] < gend[-1]).astype(I32)
    last_e = jnp.sum((gend[-1] - 1 >= gend).astype(I32))
    tile_e = jnp.where(tile_v > 0, t_e[:-1], last_e)
    return pos.astype(I32), tok, tile_base.astype(I32), tile_e.astype(I32), tile_v, R


def _rel_bucket(dist):
    n = jnp.maximum(dist, 0)
    max_exact = NUM_BUCKETS // 2
    nf = jnp.maximum(n, 1).astype(F32)
    large = max_exact + (jnp.log(nf / max_exact) / math.log(MAX_DISTANCE / max_exact)
                         * (NUM_BUCKETS - max_exact)).astype(I32)
    large = jnp.minimum(large, NUM_BUCKETS - 1)
    return jnp.where(n < max_exact, n, large)


def _bias_tables(table, t):
    assert t >= MAX_DISTANCE, "key blocks two or more away must lie entirely in the last bucket"
    d0 = jnp.arange(t, dtype=I32)[None, :] - jnp.arange(t, dtype=I32)[:, None]
    rel = (table - table[NUM_BUCKETS - 1][None, :]) * LOG2E
    out = []
    for d in (d0, d0 + t):
        bucket = _rel_bucket(d)[None]
        b = jnp.zeros((table.shape[1], t, t), F32)
        for n in range(NUM_BUCKETS):
            b = jnp.where(bucket == n, rel[n][:, None, None], b)
        out.append(jnp.where((d >= 0)[None], b, NEG))
    return jnp.stack(out, axis=1).astype(F32)


def kernel(x, rel_bias, w_in, forget_b, diff_lambda, diff_norm_g, conv_w, conv_b, conv_ln_g, conv_ln_b,
           w_out, ln1_g, ln1_b, router_w, router_b, w_gu, b_gu, w_down, b_down, ln2_g, ln2_b):
    B, L, D = x.shape
    depth = w_in.shape[0]
    assert D == D_MODEL
    T = B * L
    t = 512 if L % 512 == 0 else 256
    tm = 256
    tmm = 256
    tcv = 512 if L % 512 == 0 else 256
    assert L % t == 0 and T % tm == 0
    top_k = min(K_SEL_MAX, L // 4)
    alpha = (2 * depth) ** 0.25

    dtab_diff = _bias_tables(rel_bias[:, :N_HEADS], t)
    dtab_dsa = _bias_tables(rel_bias[:, N_HEADS:], t)

    x2d = x.reshape(T, D)
    for l in range(depth):
        lambda_init = 0.8 - 0.6 * math.exp(-0.3 * l)
        qt, ka, vt, pb, gt = _proj(x2d, _proj_weights(w_in[l]), t)

        fb_row = jnp.zeros((1, LANE), F32).at[0, :N_HEADS].set(forget_b[l])
        caug = _foxgate(pb, fb_row, B, L)
        y_fox = _fox(qt, ka, caug, vt, B, L, t)

        lp = diff_lambda[l].astype(F32)
        lam = (jnp.exp(jnp.sum(lp[0] * lp[1])) - jnp.exp(jnp.sum(lp[2] * lp[3])) + lambda_init).reshape(1)
        g_tile = jnp.broadcast_to(diff_norm_g[l].astype(F32)[:, None], (HEAD_DIM, t))
        y_diff = _diff(lam, qt, ka, vt, dtab_diff, g_tile, B, L, t, 1.0 - lambda_init)

        y_dsa = _dsa(qt, ka, vt, gt, dtab_dsa, B, L, t, top_k)
        y_conv = _conv(pb, conv_w[l], conv_b[l], conv_ln_g[l], conv_ln_b[l], B, L, tcv)

        rw = jnp.zeros((D, LANE), F32).at[:, :N_EXPERTS].set(router_w[l])
        r1 = rw.astype(BF16)
        r2 = (rw - r1.astype(F32)).astype(BF16)
        r3 = (rw - r1.astype(F32) - r2.astype(F32)).astype(BF16)
        rb = jnp.full((1, LANE), NEG, F32).at[0, :N_EXPERTS].set(router_b[l])
        x1, top_idx, top_w = _mix((y_fox, y_diff, y_dsa, y_conv), x2d, w_out[l].astype(BF16),
                                  ln1_g[l], ln1_b[l], jnp.stack([r1, r2, r3]), rb, alpha, tm)

        pos, tok, tile_base, tile_e, tile_v, R = _moe_plan(top_idx[:, :TOP_K], tmm)
        wgu_t = jnp.transpose(w_gu[l].reshape(N_EXPERTS, D // LANE, LANE, 2 * D), (0, 1, 3, 2))
        ys = _experts(tile_e, tile_v, tile_base, tok, x1, wgu_t, b_gu[l][:, None, 0::2],
                      b_gu[l][:, None, 1::2], w_down[l], b_down[l][:, None, :], R, tmm)
        x2d = _combine(pos, ys, x1, top_w, ln2_g[l], ln2_b[l], alpha, 128)
    return x2d.reshape(B, L, D)
```
